```python
import jax, jax.numpy as jnp
from jax import lax
import numpy as np

D_MODEL = 1024
BATCH = 4
SEQ = 4096
DEPTH = 1

RET_WIDTH = D_MODEL // 2
RET_HEADS = 4
RET_HEAD_DIM = RET_WIDTH // RET_HEADS
CONV_WIDTH = D_MODEL - RET_WIDTH
CONV_GROUPS = 8
CONV_K = 3
MIX_WIDTH = RET_WIDTH + CONV_WIDTH
IN_COLS = 4 * RET_WIDTH + 3 * CONV_WIDTH
CHUNK = 128
ROPE_BASE = 10000.0
N_GROUPS = 8
EXPERTS_PER_GROUP = 8
N_EXPERTS = N_GROUPS * EXPERTS_PER_GROUP
TOP_K = 2
EXPERT_FF = D_MODEL // 2
MOE_BLOCK = 128
EPS = 1e-6

kernel_name = "hybrid_retention_shortconv_hiermoe"


def rmsnorm(x, w):
    xf = x.astype(jnp.float32)
    y = xf * lax.rsqrt(jnp.mean(xf * xf, axis=-1, keepdims=True) + EPS)
    return (y * w.astype(jnp.float32)).astype(x.dtype)


def rotary(x, pos):
    half = x.shape[-1] // 2
    inv = ROPE_BASE ** (-jnp.arange(half, dtype=jnp.float32) / half)
    ang = pos.astype(jnp.float32)[:, None] * inv[None, :]
    cos = jnp.cos(ang)[None, :, None, :]
    sin = jnp.sin(ang)[None, :, None, :]
    xf = x.astype(jnp.float32)
    x1, x2 = xf[..., :half], xf[..., half:]
    return jnp.concatenate([x1 * cos - x2 * sin, x2 * cos + x1 * sin], axis=-1)


def retention_chunkwise(q, k, v):
    b, s, h, d = q.shape
    nc = s // CHUNK
    log_g = jnp.log1p(-(2.0 ** (-5.0 - jnp.arange(h, dtype=jnp.float32))))
    idx = jnp.arange(CHUNK, dtype=jnp.float32)
    rel = idx[:, None] - idx[None, :]
    causal = rel >= 0
    intra_decay = jnp.where(causal[None], jnp.exp(log_g[:, None, None] * jnp.where(causal, rel, 0.0)[None]), 0.0)
    k_decay = jnp.exp(log_g[:, None] * (CHUNK - 1 - idx)[None, :])
    q_decay = jnp.exp(log_g[:, None] * (idx + 1)[None, :])
    chunk_decay = jnp.exp(log_g * CHUNK)

    qc = q.reshape(b, nc, CHUNK, h, d)
    kc = k.reshape(b, nc, CHUNK, h, d)
    vc = v.reshape(b, nc, CHUNK, h, d)

    scores = jnp.einsum('bnihd,bnjhd->bnhij', qc, kc) * intra_decay[None, None]
    o_intra = jnp.einsum('bnhij,bnjhd->bnihd', scores, vc)

    kv = jnp.einsum('bnjhd,hj,bnjhe->bnhde', kc, k_decay, vc)

    def step(state, kv_n):
        return chunk_decay[None, :, None, None] * state + kv_n, state

    _, prev = lax.scan(step, jnp.zeros((b, h, d, d), jnp.float32), jnp.moveaxis(kv, 1, 0))
    prev = jnp.moveaxis(prev, 0, 1)
    o_inter = jnp.einsum('bnihd,bnhde->bnihe', qc, prev) * q_decay.T[None, None, :, :, None]
    return (o_intra + o_inter).reshape(b, s, h, d)


def head_groupnorm(o, w):
    b, s, h, d = o.shape
    mu = jnp.mean(o, axis=-1, keepdims=True)
    var = jnp.mean(jnp.square(o - mu), axis=-1, keepdims=True)
    y = (o - mu) * lax.rsqrt(var + EPS)
    return y.reshape(b, s, h * d) * w.astype(jnp.float32)


def causal_depthwise_conv(u, w):
    return lax.conv_general_dilated(
        u, w[:, None, :], window_strides=(1,), padding=[(CONV_K - 1, 0)],
        dimension_numbers=('NWC', 'WIO', 'NWC'), feature_group_count=u.shape[-1])


def mixer(h, w_in, ret_gn_w, conv_w, w_o):
    b, s, _ = h.shape
    proj = h @ w_in
    R, C = RET_WIDTH, CONV_WIDTH
    q, k, v, g, gb, gc, u = jnp.split(proj, [R, 2 * R, 3 * R, 4 * R, 4 * R + C, 4 * R + 2 * C], axis=-1)
    pos = jnp.arange(s)
    q = rotary(q.reshape(b, s, RET_HEADS, RET_HEAD_DIM), pos) * (RET_HEAD_DIM ** -0.5)
    k = rotary(k.reshape(b, s, RET_HEADS, RET_HEAD_DIM), pos)
    v = v.reshape(b, s, RET_HEADS, RET_HEAD_DIM).astype(jnp.float32)
    ret = retention_chunkwise(q, k, v)
    ret = (jax.nn.silu(g.astype(jnp.float32)) * head_groupnorm(ret, ret_gn_w)).astype(h.dtype)
    conv = gb * causal_depthwise_conv(gc * u, conv_w)
    return jnp.concatenate([ret, conv], axis=-1) @ w_o


def expert_dispatch(t, expert, gate, w_gate, w_up, w_down):
    n, d = t.shape
    e_total = w_gate.shape[0]
    n_slots = n * TOP_K
    flat_e = expert.reshape(-1)
    flat_tok = jnp.repeat(jnp.arange(n, dtype=jnp.int32), TOP_K)
    flat_w = gate.reshape(-1)
    order = jnp.argsort(flat_e)
    se, stok, sw = flat_e[order], flat_tok[order], flat_w[order]
    counts = jnp.bincount(flat_e, length=e_total)
    starts = jnp.cumsum(counts) - counts
    padded = ((counts + MOE_BLOCK - 1) // MOE_BLOCK) * MOE_BLOCK
    pad_ends = jnp.cumsum(padded)
    pad_starts = pad_ends - padded
    dest = pad_starts[se] + (jnp.arange(n_slots) - starts[se])
    n_blocks = -(-n_slots // MOE_BLOCK) + e_total
    cap = n_blocks * MOE_BLOCK
    buf_tok = jnp.full((cap,), n, jnp.int32).at[dest].set(stok)
    buf_w = jnp.zeros((cap,), t.dtype).at[dest].set(sw.astype(t.dtype))
    block_e = jnp.minimum(jnp.searchsorted(pad_ends, jnp.arange(n_blocks) * MOE_BLOCK, side='right'), e_total - 1)
    t_pad = jnp.concatenate([t, jnp.zeros((1, d), t.dtype)], axis=0)

    def run_block(args):
        tok, e = args
        xb = t_pad[tok]
        hb = jax.nn.silu(xb @ w_gate[e]) * (xb @ w_up[e])
        return hb @ w_down[e]

    yb = lax.map(run_block, (buf_tok.reshape(n_blocks, MOE_BLOCK), block_e))
    y = jax.ops.segment_sum(yb.reshape(cap, d) * buf_w[:, None], buf_tok, num_segments=n + 1)
    return y[:n]


def hier_moe(h, rg_w, rg_b, re_w, re_b, w_gate, w_up, w_down):
    b, s, d = h.shape
    t = h.reshape(b * s, d)
    g_prob = jax.nn.softmax((t @ rg_w).astype(jnp.float32) + rg_b.astype(jnp.float32), axis=-1)
    p_group, g_sel = lax.top_k(g_prob, 1)
    p_group, g_sel = p_group[:, 0], g_sel[:, 0]
    e_logits_all = jnp.einsum('nd,gde->nge', t, re_w).astype(jnp.float32) + re_b.astype(jnp.float32)[None]
    e_logits = jnp.take_along_axis(e_logits_all, g_sel[:, None, None], axis=1)[:, 0]
    e_prob = jax.nn.softmax(e_logits, axis=-1)
    top_p, top_i = lax.top_k(e_prob, TOP_K)
    top_p = top_p / jnp.sum(top_p, axis=-1, keepdims=True)
    gate = p_group[:, None] * top_p
    expert = g_sel[:, None] * EXPERTS_PER_GROUP + top_i
    return expert_dispatch(t, expert, gate, w_gate, w_up, w_down).reshape(b, s, d)


def setup_inputs(seed: int = 0) -> dict:
    key = jax.random.key(seed)
    ks = jax.random.split(key, 16)
    f32 = jnp.float32
    nrm = lambda k, shape, scale: jax.random.normal(k, shape, f32) * scale
    return {
        "x": nrm(ks[0], (BATCH, SEQ, D_MODEL), 1.0),
        "norm1_w": 1.0 + nrm(ks[1], (DEPTH, D_MODEL), 0.02),
        "w_in": nrm(ks[2], (DEPTH, D_MODEL, IN_COLS), D_MODEL ** -0.5),
        "ret_gn_w": 1.0 + nrm(ks[3], (DEPTH, RET_WIDTH), 0.02),
        "conv_w": nrm(ks[4], (DEPTH, CONV_K, CONV_WIDTH), CONV_K ** -0.5),
        "w_o": nrm(ks[5], (DEPTH, MIX_WIDTH, D_MODEL), MIX_WIDTH ** -0.5),
        "norm2_w": 1.0 + nrm(ks[6], (DEPTH, D_MODEL), 0.02),
        "router_g_w": nrm(ks[7], (DEPTH, D_MODEL, N_GROUPS), D_MODEL ** -0.5),
        "router_g_b": nrm(ks[8], (DEPTH, N_GROUPS), 0.01),
        "router_e_w": nrm(ks[9], (DEPTH, N_GROUPS, D_MODEL, EXPERTS_PER_GROUP), D_MODEL ** -0.5),
        "router_e_b": nrm(ks[10], (DEPTH, N_GROUPS, EXPERTS_PER_GROUP), 0.01),
        "w_gate": nrm(ks[11], (DEPTH, N_EXPERTS, D_MODEL, EXPERT_FF), D_MODEL ** -0.5),
        "w_up": nrm(ks[12], (DEPTH, N_EXPERTS, D_MODEL, EXPERT_FF), D_MODEL ** -0.5),
        "w_down": nrm(ks[13], (DEPTH, N_EXPERTS, EXPERT_FF, D_MODEL), EXPERT_FF ** -0.5),
        "final_norm_w": 1.0 + nrm(ks[14], (D_MODEL,), 0.02),
    }


def reference(x, norm1_w, w_in, ret_gn_w, conv_w, w_o, norm2_w, router_g_w, router_g_b,
              router_e_w, router_e_b, w_gate, w_up, w_down, final_norm_w):
    for l in range(DEPTH):
        h = rmsnorm(x, norm1_w[l])
        x = x + mixer(h, w_in[l], ret_gn_w[l], conv_w[l], w_o[l])
        h = rmsnorm(x, norm2_w[l])
        x = x + hier_moe(h, router_g_w[l], router_g_b[l], router_e_w[l], router_e_b[l],
                         w_gate[l], w_up[l], w_down[l])
    return rmsnorm(x, final_norm_w)
```

```python
import functools

import jax
import jax.numpy as jnp
import numpy as np
from jax import lax
from jax.experimental import pallas as pl
from jax.experimental.pallas import tpu as pltpu

F32 = jnp.float32
BF16 = jnp.bfloat16
U32 = jnp.uint32
I32 = jnp.int32

D_MODEL = 1024
RET_WIDTH = 512
RET_HEADS = 4
HEAD_DIM = 128
CONV_WIDTH = 512
CONV_K = 3
IN_COLS = 4 * RET_WIDTH + 3 * CONV_WIDTH
CHUNK = 128
ROPE_BASE = 10000.0
N_GROUPS = 8
EXPERTS_PER_GROUP = 8
N_EXPERTS = 64
TOP_K = 2
EXPERT_FF = 512
MOE_BLOCK = 128
EPS = 1e-6

LANES = 128
SUBLANES = 8
PACKED = D_MODEL // 2
EXPERT_LANE0 = N_GROUPS
VMEM_LIMIT = 48 * 1024 * 1024

MIX_TILE = 256
ROW_TILE = 512


def _pack_bf16_pair(lo, hi):
    return pltpu.pack_elementwise([lo, hi], packed_dtype=BF16)


def _unpack_bf16_pair(packed):
    lo = pltpu.unpack_elementwise(packed, index=0, packed_dtype=BF16, unpacked_dtype=F32)
    hi = pltpu.unpack_elementwise(packed, index=1, packed_dtype=BF16, unpacked_dtype=F32)
    return lo, hi


def _rms(x, w):
    ms = jnp.mean(x * x, axis=-1, keepdims=True)
    return (x * lax.rsqrt(ms + EPS)) * w


def _mixer_router_body(x_ref, n1_ref, win_ref, cos_ref, sin_ref, dmat_ref, kdec_ref, qdec_ref,
                       gnw_ref, convw_ref, wo_ref, n2_ref, wrh_ref, wrl_ref, rb_ref,
                       x1_ref, h2p_ref, route_ref, cnt_ref,
                       state_ref, cu_ref, mix_ref, run_ref, *, chunk_decay):
    b = pl.program_id(0)
    c = pl.program_id(1)
    tm = x_ref.shape[0]

    @pl.when(c == 0)
    def _():
        state_ref[...] = jnp.zeros_like(state_ref)
        cu_ref[0:SUBLANES, :] = jnp.zeros((SUBLANES, CONV_WIDTH), F32)

    @pl.when((b == 0) & (c == 0))
    def _():
        run_ref[...] = jnp.zeros_like(run_ref)

    x = x_ref[...]
    h = _rms(x, n1_ref[...])
    proj = jnp.dot(h.astype(BF16), win_ref[...], preferred_element_type=F32)

    R = RET_WIDTH
    cosf = cos_ref[...]
    sins = sin_ref[...]

    for j in range(tm // CHUNK):
        r0 = j * CHUNK
        for hh in range(RET_HEADS):
            c0 = hh * HEAD_DIM
            qh = proj[r0:r0 + CHUNK, c0:c0 + HEAD_DIM]
            kh = proj[r0:r0 + CHUNK, R + c0:R + c0 + HEAD_DIM]
            vh = proj[r0:r0 + CHUNK, 2 * R + c0:2 * R + c0 + HEAD_DIM]
            gh = proj[r0:r0 + CHUNK, 3 * R + c0:3 * R + c0 + HEAD_DIM]
            cs = cosf[r0:r0 + CHUNK, :]
            sn = sins[r0:r0 + CHUNK, :]
            qr = qh * cs + pltpu.roll(qh, HEAD_DIM // 2, 1) * sn
            kr = kh * cs + pltpu.roll(kh, HEAD_DIM // 2, 1) * sn
            qb = qr.astype(BF16)
            kb = kr.astype(BF16)
            vb = vh.astype(BF16)
            scores = lax.dot_general(qb, kb, (((1,), (1,)), ((), ())), preferred_element_type=F32)
            scores = scores * dmat_ref[hh]
            o = jnp.dot(scores.astype(BF16), vb, preferred_element_type=F32)
            s_prev = state_ref[hh]
            qd = (qr * qdec_ref[:, c0:c0 + HEAD_DIM]).astype(BF16)
            o = o + jnp.dot(qd, s_prev.astype(BF16), preferred_element_type=F32)
            kd = (kr * kdec_ref[:, c0:c0 + HEAD_DIM]).astype(BF16)
            kv = lax.dot_general(kd, vb, (((0,), (0,)), ((), ())), preferred_element_type=F32)
            state_ref[hh] = chunk_decay[hh] * s_prev + kv
            mu = jnp.mean(o, axis=-1, keepdims=True)
            d = o - mu
            var = jnp.mean(d * d, axis=-1, keepdims=True)
            y = d * lax.rsqrt(var + EPS) * gnw_ref[:, c0:c0 + HEAD_DIM]
            gate = gh * (1.0 / (1.0 + jnp.exp(-gh)))
            mix_ref[r0:r0 + CHUNK, c0:c0 + HEAD_DIM] = (gate * y).astype(BF16)

    gb = proj[:, 4 * R:4 * R + CONV_WIDTH]
    gc = proj[:, 4 * R + CONV_WIDTH:4 * R + 2 * CONV_WIDTH]
    u = proj[:, 4 * R + 2 * CONV_WIDTH:4 * R + 3 * CONV_WIDTH]
    cu_ref[SUBLANES:SUBLANES + tm, :] = gc * u
    conv = (convw_ref[2:3, :] * cu_ref[SUBLANES:SUBLANES + tm, :]
            + convw_ref[1:2, :] * cu_ref[SUBLANES - 1:SUBLANES - 1 + tm, :]
            + convw_ref[0:1, :] * cu_ref[SUBLANES - 2:SUBLANES - 2 + tm, :])
    mix_ref[:, R:R + CONV_WIDTH] = (gb * conv).astype(BF16)
    cu_ref[0:SUBLANES, :] = cu_ref[tm:tm + SUBLANES, :]

    x1 = x + jnp.dot(mix_ref[...], wo_ref[...], preferred_element_type=F32)
    x1_ref[...] = x1
    h2 = _rms(x1, n2_ref[...])
    h2p_ref[...] = _pack_bf16_pair(h2[:, :PACKED], h2[:, PACKED:])

    hi = h2.astype(BF16)
    lo = (h2 - hi.astype(F32)).astype(BF16)
    wrh = wrh_ref[...]
    logits = (jnp.dot(hi, wrh, preferred_element_type=F32)
              + jnp.dot(lo, wrh, preferred_element_type=F32)
              + jnp.dot(hi, wrl_ref[...], preferred_element_type=F32)
              + rb_ref[...])

    lane = lax.broadcasted_iota(I32, (tm, LANES), 1)
    lane_f = lane.astype(F32)
    neg_inf = jnp.float32(-jnp.inf)

    gmask = lane < N_GROUPS
    gl = jnp.where(gmask, logits, neg_inf)
    ge = jnp.exp(gl - jnp.max(gl, axis=-1, keepdims=True))
    gp = ge / jnp.sum(ge, axis=-1, keepdims=True)
    p_group = jnp.max(gp, axis=-1, keepdims=True)
    g_sel = jnp.min(jnp.where(gmask & (gp == p_group), lane_f, float(LANES)), axis=-1, keepdims=True)

    e_lo = EXPERT_LANE0 + g_sel * EXPERTS_PER_GROUP
    emask = (lane_f >= e_lo) & (lane_f < e_lo + EXPERTS_PER_GROUP)
    el = jnp.where(emask, logits, neg_inf)
    ee = jnp.exp(el - jnp.max(el, axis=-1, keepdims=True))
    ep = jnp.where(emask, ee / jnp.sum(ee, axis=-1, keepdims=True), -1.0)
    p1 = jnp.max(ep, axis=-1, keepdims=True)
    i1 = jnp.min(jnp.where(ep == p1, lane_f, float(LANES)), axis=-1, keepdims=True)
    ep2 = jnp.where(lane_f == i1, -1.0, ep)
    p2 = jnp.max(ep2, axis=-1, keepdims=True)
    i2 = jnp.min(jnp.where(ep2 == p2, lane_f, float(LANES)), axis=-1, keepdims=True)
    psum = p1 + p2
    gate1 = p_group * (p1 / psum)
    gate2 = p_group * (p2 / psum)

    oh1 = jnp.where(lane_f == i1, 1.0, 0.0)
    oh2 = jnp.where(lane_f == i2, 1.0, 0.0)
    row = lax.broadcasted_iota(I32, (tm, tm), 0)
    col = lax.broadcasted_iota(I32, (tm, tm), 1)
    lower = jnp.where(row > col, 1.0, 0.0).astype(BF16)
    pre1 = jnp.dot(lower, oh1.astype(BF16), preferred_element_type=F32)
    pre2 = jnp.dot(lower, oh2.astype(BF16), preferred_element_type=F32)
    cnt1 = jnp.sum(oh1, axis=0, keepdims=True)
    cnt2 = jnp.sum(oh2, axis=0, keepdims=True)
    run = run_ref[...]
    rank1 = jnp.sum(oh1 * (pre1 + run), axis=-1, keepdims=True)
    rank2 = jnp.sum(oh2 * (pre2 + run + cnt1), axis=-1, keepdims=True)
    run_new = run + cnt1 + cnt2
    run_ref[...] = run_new
    cnt_ref[...] = run_new

    rl = lax.broadcasted_iota(I32, (tm, SUBLANES), 1)
    route = jnp.where(rl == 0, i1 - EXPERT_LANE0,
            jnp.where(rl == 1, i2 - EXPERT_LANE0,
            jnp.where(rl == 2, gate1,
            jnp.where(rl == 3, gate2,
            jnp.where(rl == 4, rank1,
            jnp.where(rl == 5, rank2, 0.0))))))
    route_ref[...] = route


def _mixer_router(x, n1, win_bf, cosf, sins, dmat, kdec, qdec, gnw, convw, wo_bf, n2, wrh, wrl, rb,
                  chunk_decay):
    B, S, D = x.shape
    tm = MIX_TILE
    grid = (B, S // tm)
    const2 = lambda b, c: (0, 0)
    in_specs = [
        pl.BlockSpec((None, tm, D), lambda b, c: (b, c, 0)),
        pl.BlockSpec((1, D), const2),
        pl.BlockSpec((D, IN_COLS), const2),
        pl.BlockSpec((tm, HEAD_DIM), lambda b, c: (c, 0)),
        pl.BlockSpec((tm, HEAD_DIM), lambda b, c: (c, 0)),
        pl.BlockSpec((RET_HEADS, CHUNK, CHUNK), lambda b, c: (0, 0, 0)),
        pl.BlockSpec((CHUNK, RET_WIDTH), const2),
        pl.BlockSpec((CHUNK, RET_WIDTH), const2),
        pl.BlockSpec((1, RET_WIDTH), const2),
        pl.BlockSpec((CONV_K, CONV_WIDTH), const2),
        pl.BlockSpec((D, D), const2),
        pl.BlockSpec((1, D), const2),
        pl.BlockSpec((D, LANES), const2),
        pl.BlockSpec((D, LANES), const2),
        pl.BlockSpec((1, LANES), const2),
    ]
    out_specs = [
        pl.BlockSpec((None, tm, D), lambda b, c: (b, c, 0)),
        pl.BlockSpec((None, tm, PACKED), lambda b, c: (b, c, 0)),
        pl.BlockSpec((None, tm, SUBLANES), lambda b, c: (b, c, 0)),
        pl.BlockSpec((1, LANES), const2),
    ]
    out_shape = [
        jax.ShapeDtypeStruct((B, S, D), F32),
        jax.ShapeDtypeStruct((B, S, PACKED), U32),
        jax.ShapeDtypeStruct((B, S, SUBLANES), F32),
        jax.ShapeDtypeStruct((1, LANES), F32),
    ]
    scratch = [
        pltpu.VMEM((RET_HEADS, HEAD_DIM, HEAD_DIM), F32),
        pltpu.VMEM((tm + SUBLANES, CONV_WIDTH), F32),
        pltpu.VMEM((tm, D), BF16),
        pltpu.VMEM((1, LANES), F32),
    ]
    return pl.pallas_call(
        functools.partial(_mixer_router_body, chunk_decay=chunk_decay),
        grid=grid, in_specs=in_specs, out_specs=out_specs, out_shape=out_shape,
        scratch_shapes=scratch,
        compiler_params=pltpu.CompilerParams(
            dimension_semantics=("arbitrary", "arbitrary"), vmem_limit_bytes=VMEM_LIMIT),
        name="mixer_router",
    )(x, n1, win_bf, cosf, sins, dmat, kdec, qdec, gnw, convw, wo_bf, n2, wrh, wrl, rb)


def _row_copy(src_ref, src_row, dst_ref, dst_row, sem):
    return pltpu.make_async_copy(src_ref.at[pl.ds(src_row, 1)], dst_ref.at[pl.ds(dst_row, 1)], sem)


def _dispatch_body(dest_ref, h2p_ref, xs_in_ref, xs_ref, sem):
    del xs_in_ref
    tt = h2p_ref.shape[0]

    def issue(t, carry):
        _row_copy(h2p_ref, t, xs_ref, dest_ref[2 * t], sem).start()
        _row_copy(h2p_ref, t, xs_ref, dest_ref[2 * t + 1], sem).start()
        return carry

    lax.fori_loop(0, tt, issue, 0, unroll=8)

    def drain(t, carry):
        _row_copy(h2p_ref, 0, xs_ref, 0, sem).wait()
        _row_copy(h2p_ref, 0, xs_ref, 0, sem).wait()
        return carry

    lax.fori_loop(0, tt, drain, 0, unroll=8)


def _dispatch(dest_flat, h2p, xs_init):
    n = h2p.shape[0]
    tt = ROW_TILE
    return pl.pallas_call(
        _dispatch_body,
        grid=(n // tt,),
        in_specs=[
            pl.BlockSpec((TOP_K * tt,), lambda i: (i,), memory_space=pltpu.SMEM),
            pl.BlockSpec((tt, PACKED), lambda i: (i, 0)),
            pl.BlockSpec(memory_space=pl.ANY),
        ],
        out_specs=pl.BlockSpec(memory_space=pl.ANY),
        out_shape=jax.ShapeDtypeStruct(xs_init.shape, U32),
        scratch_shapes=[pltpu.SemaphoreType.DMA],
        input_output_aliases={2: 0},
        compiler_params=pltpu.CompilerParams(dimension_semantics=("arbitrary",)),
        name="dispatch",
    )(dest_flat, h2p, xs_init)


def _experts_body(be_ref, nb_ref, xs_ref, wg_ref, wu_ref, wd_ref, ys_ref, wgb_ref, wub_ref, wdb_ref):
    i = pl.program_id(0)
    prev = be_ref[jnp.maximum(i - 1, 0)]

    @pl.when(i < nb_ref[0])
    def _():
        @pl.when((i == 0) | (be_ref[i] != prev))
        def _():
            wgb_ref[...] = wg_ref[...].astype(BF16)
            wub_ref[...] = wu_ref[...].astype(BF16)
            wdb_ref[...] = wd_ref[...].astype(BF16)

        lo, hi = _unpack_bf16_pair(xs_ref[...])
        xb = jnp.concatenate([lo.astype(BF16), hi.astype(BF16)], axis=-1)
        g = jnp.dot(xb, wgb_ref[...], preferred_element_type=F32)
        u = jnp.dot(xb, wub_ref[...], preferred_element_type=F32)
        hmid = (g * (1.0 / (1.0 + jnp.exp(-g)))) * u
        y = jnp.dot(hmid.astype(BF16), wdb_ref[...], preferred_element_type=F32)
        ys_ref[...] = _pack_bf16_pair(y[:, :PACKED], y[:, PACKED:])

    @pl.when(i >= nb_ref[0])
    def _():
        zero = jnp.zeros(ys_ref.shape, F32)
        ys_ref[...] = _pack_bf16_pair(zero, zero)


def _experts(block_e, n_active, xs, w_gate, w_up, w_down):
    cap = xs.shape[0]
    n_blocks = cap // MOE_BLOCK
    row_map = lambda i, be, nb: (jnp.minimum(i, nb[0] - 1), 0)
    w_map = lambda i, be, nb: (be[i], 0, 0)
    return pl.pallas_call(
        _experts_body,
        grid_spec=pltpu.PrefetchScalarGridSpec(
            num_scalar_prefetch=2,
            grid=(n_blocks,),
            in_specs=[
                pl.BlockSpec((MOE_BLOCK, PACKED), row_map),
                pl.BlockSpec((None, D_MODEL, EXPERT_FF), w_map),
                pl.BlockSpec((None, D_MODEL, EXPERT_FF), w_map),
                pl.BlockSpec((None, EXPERT_FF, D_MODEL), w_map),
            ],
            out_specs=pl.BlockSpec((MOE_BLOCK, PACKED), lambda i, be, nb: (i, 0)),
            scratch_shapes=[
                pltpu.VMEM((D_MODEL, EXPERT_FF), BF16),
                pltpu.VMEM((D_MODEL, EXPERT_FF), BF16),
                pltpu.VMEM((EXPERT_FF, D_MODEL), BF16),
            ],
        ),
        out_shape=jax.ShapeDtypeStruct((cap, PACKED), U32),
        compiler_params=pltpu.CompilerParams(
            dimension_semantics=("arbitrary",), vmem_limit_bytes=VMEM_LIMIT),
        name="experts",
    )(block_e, n_active, xs, w_gate, w_up, w_down)


def _combine_body(dest_ref, x1_ref, route_ref, fw_ref, ys_ref, out_ref, y0_ref, y1_ref, sem):
    tt = x1_ref.shape[0]

    def issue(t, carry):
        _row_copy(ys_ref, dest_ref[2 * t], y0_ref, t, sem).start()
        _row_copy(ys_ref, dest_ref[2 * t + 1], y1_ref, t, sem).start()
        return carry

    lax.fori_loop(0, tt, issue, 0, unroll=8)

    def drain(t, carry):
        _row_copy(ys_ref, 0, y0_ref, 0, sem).wait()
        _row_copy(ys_ref, 0, y1_ref, 0, sem).wait()
        return carry

    lax.fori_loop(0, tt, drain, 0, unroll=8)

    route = route_ref[...]
    g0 = route[:, 2:3]
    g1 = route[:, 3:4]
    a_lo, a_hi = _unpack_bf16_pair(y0_ref[...])
    b_lo, b_hi = _unpack_bf16_pair(y1_ref[...])
    x1 = x1_ref[...]
    z_lo = x1[:, :PACKED] + (a_lo * g0 + b_lo * g1)
    z_hi = x1[:, PACKED:] + (a_hi * g0 + b_hi * g1)
    ms = (jnp.sum(z_lo * z_lo, axis=-1, keepdims=True)
          + jnp.sum(z_hi * z_hi, axis=-1, keepdims=True)) * (1.0 / D_MODEL)
    inv = lax.rsqrt(ms + EPS)
    out_ref[:, :PACKED] = (z_lo * inv) * fw_ref[:, :PACKED]
    out_ref[:, PACKED:] = (z_hi * inv) * fw_ref[:, PACKED:]


def _combine(dest_flat, x1, route, fw, ys):
    n, d = x1.shape
    tt = ROW_TILE
    return pl.pallas_call(
        _combine_body,
        grid=(n // tt,),
        in_specs=[
            pl.BlockSpec((TOP_K * tt,), lambda i: (i,), memory_space=pltpu.SMEM),
            pl.BlockSpec((tt, d), lambda i: (i, 0)),
            pl.BlockSpec((tt, SUBLANES), lambda i: (i, 0)),
            pl.BlockSpec((1, d), lambda i: (0, 0)),
            pl.BlockSpec(memory_space=pl.ANY),
        ],
        out_specs=pl.BlockSpec((tt, d), lambda i: (i, 0)),
        out_shape=jax.ShapeDtypeStruct((n, d), F32),
        scratch_shapes=[
            pltpu.VMEM((tt, PACKED), U32),
            pltpu.VMEM((tt, PACKED), U32),
            pltpu.SemaphoreType.DMA,
        ],
        compiler_params=pltpu.CompilerParams(
            dimension_semantics=("arbitrary",), vmem_limit_bytes=VMEM_LIMIT),
        name="combine",
    )(dest_flat, x1, route, fw, ys)


def _retention_tables(seq):
    half = HEAD_DIM // 2
    inv = ROPE_BASE ** (-jnp.arange(half, dtype=F32) / half)
    ang = jnp.arange(seq).astype(F32)[:, None] * inv[None, :]
    cos, sin = jnp.cos(ang), jnp.sin(ang)
    cosf = jnp.concatenate([cos, cos], axis=-1)
    sins = jnp.concatenate([-sin, sin], axis=-1)
    log_g = jnp.log1p(-(2.0 ** (-5.0 - jnp.arange(RET_HEADS, dtype=F32))))
    idx = jnp.arange(CHUNK, dtype=F32)
    rel = idx[:, None] - idx[None, :]
    causal = rel >= 0
    scale = HEAD_DIM ** -0.5
    dmat = jnp.where(causal[None], jnp.exp(log_g[:, None, None] * jnp.where(causal, rel, 0.0)[None]), 0.0)
    k_decay = jnp.exp(log_g[:, None] * (CHUNK - 1 - idx)[None, :])
    q_decay = jnp.exp(log_g[:, None] * (idx + 1)[None, :])
    kdec = jnp.repeat(k_decay.T, HEAD_DIM, axis=1)
    qdec = jnp.repeat(q_decay.T, HEAD_DIM, axis=1) * scale
    return cosf, sins, dmat * scale, kdec, qdec


def _chunk_decay():
    log_g = np.log1p(-(2.0 ** (-5.0 - np.arange(RET_HEADS, dtype=np.float64))))
    return tuple(float(v) for v in np.exp(log_g * CHUNK))


def _layer(x, norm1_w, w_in, ret_gn_w, conv_w, w_o, norm2_w, router_g_w, router_g_b,
           router_e_w, router_e_b, w_gate, w_up, w_down, final_w):
    B, S, D = x.shape
    n = B * S
    cosf, sins, dmat, kdec, qdec = _retention_tables(S)

    wr = jnp.zeros((D, LANES), F32)
    wr = wr.at[:, :N_GROUPS].set(router_g_w)
    wr = wr.at[:, EXPERT_LANE0:EXPERT_LANE0 + N_EXPERTS].set(
        jnp.transpose(router_e_w, (1, 0, 2)).reshape(D, N_EXPERTS))
    wrh = wr.astype(BF16)
    wrl = (wr - wrh.astype(F32)).astype(BF16)
    rb = jnp.zeros((1, LANES), F32)
    rb = rb.at[0, :N_GROUPS].set(router_g_b)
    rb = rb.at[0, EXPERT_LANE0:EXPERT_LANE0 + N_EXPERTS].set(router_e_b.reshape(-1))

    x1, h2p, route, counts = _mixer_router(
        x, norm1_w[None], w_in.astype(BF16), cosf, sins, dmat, kdec, qdec, ret_gn_w[None], conv_w,
        w_o.astype(BF16), norm2_w[None], wrh, wrl, rb, _chunk_decay())

    route = route.reshape(n, SUBLANES)
    expert = route[:, 0:2].astype(I32)
    rank = route[:, 4:6].astype(I32)
    cnt = counts[0, EXPERT_LANE0:EXPERT_LANE0 + N_EXPERTS].astype(I32)
    padded = ((cnt + MOE_BLOCK - 1) // MOE_BLOCK) * MOE_BLOCK
    pad_ends = jnp.cumsum(padded)
    pad_starts = pad_ends - padded
    dest = (pad_starts[expert] + rank).reshape(-1)
    n_blocks = -(-(n * TOP_K) // MOE_BLOCK) + N_EXPERTS
    n_active = (pad_ends[-1] // MOE_BLOCK).astype(I32)
    blk = jnp.minimum(jnp.arange(n_blocks, dtype=I32), n_active - 1) * MOE_BLOCK
    block_e = jnp.sum((pad_ends[None, :] <= blk[:, None]).astype(I32), axis=1)
    block_e = jnp.minimum(block_e, N_EXPERTS - 1)

    xs = _dispatch(dest, h2p.reshape(n, PACKED), jnp.zeros((n_blocks * MOE_BLOCK, PACKED), U32))
    ys = _experts(block_e, n_active.reshape(1), xs, w_gate, w_up, w_down)
    out = _combine(dest, x1.reshape(n, D), route, final_w[None], ys)
    return out.reshape(B, S, D)


def kernel(x, norm1_w, w_in, ret_gn_w, conv_w, w_o, norm2_w, router_g_w, router_g_b, router_e_w,
           router_e_b, w_gate, w_up, w_down, final_norm_w):
    depth = norm1_w.shape[0]
    assert depth == 1, "the final RMSNorm is fused into the last layer's combine step"
    return _layer(x, norm1_w[0], w_in[0], ret_gn_w[0], conv_w[0], w_o[0], norm2_w[0],
                  router_g_w[0], router_g_b[0], router_e_w[0], router_e_b[0],
                  w_gate[0], w_up[0], w_down[0], final_norm_w)
```

```python
import functools

import jax
import jax.numpy as jnp
import numpy as np
from jax import lax
from jax.experimental import pallas as pl
from jax.experimental.pallas import tpu as pltpu

F32 = jnp.float32
BF16 = jnp.bfloat16
U32 = jnp.uint32
I32 = jnp.int32

D_MODEL = 1024
RET_WIDTH = 512
RET_HEADS = 4
HEAD_DIM = 128
CONV_WIDTH = 512
CONV_K = 3
IN_COLS = 4 * RET_WIDTH + 3 * CONV_WIDTH
CHUNK = 128
ROPE_BASE = 10000.0
N_GROUPS = 8
EXPERTS_PER_GROUP = 8
N_EXPERTS = 64
TOP_K = 2
EXPERT_FF = 512
MOE_BLOCK = 128
EPS = 1e-6

LANES = 128
SUBLANES = 8
PACKED = D_MODEL // 2
EXPERT_LANE0 = N_GROUPS
VMEM_LIMIT = 48 * 1024 * 1024

MIX_TILE = 256
ROW_TILE = 1024
CHUNK_BLOCKS = 4
CHUNK_ROWS = CHUNK_BLOCKS * MOE_BLOCK


def _pack_bf16_pair(lo, hi):
    return pltpu.pack_elementwise([lo, hi], packed_dtype=BF16)


def _unpack_bf16_pair(packed):
    lo = pltpu.unpack_elementwise(packed, index=0, packed_dtype=BF16, unpacked_dtype=F32)
    hi = pltpu.unpack_elementwise(packed, index=1, packed_dtype=BF16, unpacked_dtype=F32)
    return lo, hi


def _rms(x, w):
    ms = jnp.mean(x * x, axis=-1, keepdims=True)
    return (x * lax.rsqrt(ms + EPS)) * w


def _mixer_router_body(x_ref, n1_ref, win_ref, cos_ref, sin_ref, dmat_ref, kdec_ref, qdec_ref,
                       gnw_ref, convw_ref, wo_ref, n2_ref, wrh_ref, wrl_ref, rb_ref,
                       x1_ref, h2p_ref, route_ref, route_t_ref, cnt_ref,
                       state_ref, cu_ref, mix_ref, run_ref, *, chunk_decay):
    b = pl.program_id(0)
    c = pl.program_id(1)
    tm = x_ref.shape[0]

    @pl.when(c == 0)
    def _():
        state_ref[...] = jnp.zeros_like(state_ref)
        cu_ref[0:SUBLANES, :] = jnp.zeros((SUBLANES, CONV_WIDTH), F32)

    @pl.when((b == 0) & (c == 0))
    def _():
        run_ref[...] = jnp.zeros_like(run_ref)

    x = x_ref[...]
    h = _rms(x, n1_ref[...])
    proj = jnp.dot(h.astype(BF16), win_ref[...], preferred_element_type=F32)

    R = RET_WIDTH
    cosf = cos_ref[...]
    sins = sin_ref[...]

    for j in range(tm // CHUNK):
        r0 = j * CHUNK
        for hh in range(RET_HEADS):
            c0 = hh * HEAD_DIM
            qh = proj[r0:r0 + CHUNK, c0:c0 + HEAD_DIM]
            kh = proj[r0:r0 + CHUNK, R + c0:R + c0 + HEAD_DIM]
            vh = proj[r0:r0 + CHUNK, 2 * R + c0:2 * R + c0 + HEAD_DIM]
            gh = proj[r0:r0 + CHUNK, 3 * R + c0:3 * R + c0 + HEAD_DIM]
            cs = cosf[r0:r0 + CHUNK, :]
            sn = sins[r0:r0 + CHUNK, :]
            qr = qh * cs + pltpu.roll(qh, HEAD_DIM // 2, 1) * sn
            kr = kh * cs + pltpu.roll(kh, HEAD_DIM // 2, 1) * sn
            qb = qr.astype(BF16)
            kb = kr.astype(BF16)
            vb = vh.astype(BF16)
            scores = lax.dot_general(qb, kb, (((1,), (1,)), ((), ())), preferred_element_type=F32)
            scores = scores * dmat_ref[hh]
            o = jnp.dot(scores.astype(BF16), vb, preferred_element_type=F32)
            s_prev = state_ref[hh]
            qd = (qr * qdec_ref[:, c0:c0 + HEAD_DIM]).astype(BF16)
            o = o + jnp.dot(qd, s_prev.astype(BF16), preferred_element_type=F32)
            kd = (kr * kdec_ref[:, c0:c0 + HEAD_DIM]).astype(BF16)
            kv = lax.dot_general(kd, vb, (((0,), (0,)), ((), ())), preferred_element_type=F32)
            state_ref[hh] = chunk_decay[hh] * s_prev + kv
            mu = jnp.mean(o, axis=-1, keepdims=True)
            d = o - mu
            var = jnp.mean(d * d, axis=-1, keepdims=True)
            y = d * lax.rsqrt(var + EPS) * gnw_ref[:, c0:c0 + HEAD_DIM]
            gate = gh * (1.0 / (1.0 + jnp.exp(-gh)))
            mix_ref[r0:r0 + CHUNK, c0:c0 + HEAD_DIM] = (gate * y).astype(BF16)

    gb = proj[:, 4 * R:4 * R + CONV_WIDTH]
    gc = proj[:, 4 * R + CONV_WIDTH:4 * R + 2 * CONV_WIDTH]
    u = proj[:, 4 * R + 2 * CONV_WIDTH:4 * R + 3 * CONV_WIDTH]
    cu_ref[SUBLANES:SUBLANES + tm, :] = gc * u
    conv = (convw_ref[2:3, :] * cu_ref[SUBLANES:SUBLANES + tm, :]
            + convw_ref[1:2, :] * cu_ref[SUBLANES - 1:SUBLANES - 1 + tm, :]
            + convw_ref[0:1, :] * cu_ref[SUBLANES - 2:SUBLANES - 2 + tm, :])
    mix_ref[:, R:R + CONV_WIDTH] = (gb * conv).astype(BF16)
    cu_ref[0:SUBLANES, :] = cu_ref[tm:tm + SUBLANES, :]

    x1 = x + jnp.dot(mix_ref[...], wo_ref[...], preferred_element_type=F32)
    x1_ref[...] = x1
    h2 = _rms(x1, n2_ref[...])
    h2p_ref[...] = _pack_bf16_pair(h2[:, :PACKED], h2[:, PACKED:])

    hi = h2.astype(BF16)
    lo = (h2 - hi.astype(F32)).astype(BF16)
    wrh = wrh_ref[...]
    logits = (jnp.dot(hi, wrh, preferred_element_type=F32)
              + jnp.dot(lo, wrh, preferred_element_type=F32)
              + jnp.dot(hi, wrl_ref[...], preferred_element_type=F32)
              + rb_ref[...])

    lane = lax.broadcasted_iota(I32, (tm, LANES), 1)
    lane_f = lane.astype(F32)
    neg_inf = jnp.float32(-jnp.inf)

    gmask = lane < N_GROUPS
    gl = jnp.where(gmask, logits, neg_inf)
    ge = jnp.exp(gl - jnp.max(gl, axis=-1, keepdims=True))
    gp = ge / jnp.sum(ge, axis=-1, keepdims=True)
    p_group = jnp.max(gp, axis=-1, keepdims=True)
    g_sel = jnp.min(jnp.where(gmask & (gp == p_group), lane_f, float(LANES)), axis=-1, keepdims=True)

    e_lo = EXPERT_LANE0 + g_sel * EXPERTS_PER_GROUP
    emask = (lane_f >= e_lo) & (lane_f < e_lo + EXPERTS_PER_GROUP)
    el = jnp.where(emask, logits, neg_inf)
    ee = jnp.exp(el - jnp.max(el, axis=-1, keepdims=True))
    ep = jnp.where(emask, ee / jnp.sum(ee, axis=-1, keepdims=True), -1.0)
    p1 = jnp.max(ep, axis=-1, keepdims=True)
    i1 = jnp.min(jnp.where(ep == p1, lane_f, float(LANES)), axis=-1, keepdims=True)
    ep2 = jnp.where(lane_f == i1, -1.0, ep)
    p2 = jnp.max(ep2, axis=-1, keepdims=True)
    i2 = jnp.min(jnp.where(ep2 == p2, lane_f, float(LANES)), axis=-1, keepdims=True)
    psum = p1 + p2
    gate1 = p_group * (p1 / psum)
    gate2 = p_group * (p2 / psum)

    oh1 = jnp.where(lane_f == i1, 1.0, 0.0)
    oh2 = jnp.where(lane_f == i2, 1.0, 0.0)
    row = lax.broadcasted_iota(I32, (tm, tm), 0)
    col = lax.broadcasted_iota(I32, (tm, tm), 1)
    lower = jnp.where(row > col, 1.0, 0.0).astype(BF16)
    pre1 = jnp.dot(lower, oh1.astype(BF16), preferred_element_type=F32)
    pre2 = jnp.dot(lower, oh2.astype(BF16), preferred_element_type=F32)
    cnt1 = jnp.sum(oh1, axis=0, keepdims=True)
    cnt2 = jnp.sum(oh2, axis=0, keepdims=True)
    run = run_ref[...]
    rank1 = jnp.sum(oh1 * (pre1 + run), axis=-1, keepdims=True)
    rank2 = jnp.sum(oh2 * (pre2 + run + cnt1), axis=-1, keepdims=True)
    run_new = run + cnt1 + cnt2
    run_ref[...] = run_new
    cnt_ref[...] = run_new

    route = jnp.where(lane == 0, i1 - EXPERT_LANE0,
            jnp.where(lane == 1, i2 - EXPERT_LANE0,
            jnp.where(lane == 2, gate1,
            jnp.where(lane == 3, gate2,
            jnp.where(lane == 4, rank1,
            jnp.where(lane == 5, rank2, 0.0))))))
    route_ref[...] = route[:, :SUBLANES]
    route_t_ref[...] = jnp.transpose(route)[:SUBLANES, :]


def _mixer_router(x, n1, win_bf, cosf, sins, dmat, kdec, qdec, gnw, convw, wo_bf, n2, wrh, wrl, rb,
                  chunk_decay):
    B, S, D = x.shape
    tm = MIX_TILE
    grid = (B, S // tm)
    const2 = lambda b, c: (0, 0)
    in_specs = [
        pl.BlockSpec((None, tm, D), lambda b, c: (b, c, 0)),
        pl.BlockSpec((1, D), const2),
        pl.BlockSpec((D, IN_COLS), const2),
        pl.BlockSpec((tm, HEAD_DIM), lambda b, c: (c, 0)),
        pl.BlockSpec((tm, HEAD_DIM), lambda b, c: (c, 0)),
        pl.BlockSpec((RET_HEADS, CHUNK, CHUNK), lambda b, c: (0, 0, 0)),
        pl.BlockSpec((CHUNK, RET_WIDTH), const2),
        pl.BlockSpec((CHUNK, RET_WIDTH), const2),
        pl.BlockSpec((1, RET_WIDTH), const2),
        pl.BlockSpec((CONV_K, CONV_WIDTH), const2),
        pl.BlockSpec((D, D), const2),
        pl.BlockSpec((1, D), const2),
        pl.BlockSpec((D, LANES), const2),
        pl.BlockSpec((D, LANES), const2),
        pl.BlockSpec((1, LANES), const2),
    ]
    out_specs = [
        pl.BlockSpec((None, tm, D), lambda b, c: (b, c, 0)),
        pl.BlockSpec((None, tm, PACKED), lambda b, c: (b, c, 0)),
        pl.BlockSpec((None, tm, SUBLANES), lambda b, c: (b, c, 0)),
        pl.BlockSpec((SUBLANES, tm), lambda b, c: (0, b * (S // tm) + c)),
        pl.BlockSpec((1, LANES), const2),
    ]
    out_shape = [
        jax.ShapeDtypeStruct((B, S, D), F32),
        jax.ShapeDtypeStruct((B, S, PACKED), U32),
        jax.ShapeDtypeStruct((B, S, SUBLANES), F32),
        jax.ShapeDtypeStruct((SUBLANES, B * S), F32),
        jax.ShapeDtypeStruct((1, LANES), F32),
    ]
    scratch = [
        pltpu.VMEM((RET_HEADS, HEAD_DIM, HEAD_DIM), F32),
        pltpu.VMEM((tm + SUBLANES, CONV_WIDTH), F32),
        pltpu.VMEM((tm, D), BF16),
        pltpu.VMEM((1, LANES), F32),
    ]
    return pl.pallas_call(
        functools.partial(_mixer_router_body, chunk_decay=chunk_decay),
        grid=grid, in_specs=in_specs, out_specs=out_specs, out_shape=out_shape,
        scratch_shapes=scratch,
        compiler_params=pltpu.CompilerParams(
            dimension_semantics=("arbitrary", "arbitrary"), vmem_limit_bytes=VMEM_LIMIT),
        name="mixer_router",
    )(x, n1, win_bf, cosf, sins, dmat, kdec, qdec, gnw, convw, wo_bf, n2, wrh, wrl, rb)


def _row_copy(src_ref, src_row, dst_ref, dst_row, sem):
    return pltpu.make_async_copy(src_ref.at[pl.ds(src_row, 1)], dst_ref.at[pl.ds(dst_row, 1)], sem)


def _dispatch_body(dest0_ref, dest1_ref, h2p_ref, xs_in_ref, xs_ref, sem):
    del xs_in_ref
    tt = h2p_ref.shape[0]

    def issue(t, carry):
        _row_copy(h2p_ref, t, xs_ref, dest0_ref[t], sem).start()
        _row_copy(h2p_ref, t, xs_ref, dest1_ref[t], sem).start()
        return carry

    lax.fori_loop(0, tt, issue, 0, unroll=8)

    def drain(t, carry):
        _row_copy(h2p_ref, 0, xs_ref, 0, sem).wait()
        _row_copy(h2p_ref, 0, xs_ref, 0, sem).wait()
        return carry

    lax.fori_loop(0, tt, drain, 0, unroll=8)


def _dispatch(dest0, dest1, h2p, xs_init):
    n = h2p.shape[0]
    tt = ROW_TILE
    return pl.pallas_call(
        _dispatch_body,
        grid=(n // tt,),
        in_specs=[
            pl.BlockSpec((tt,), lambda i: (i,), memory_space=pltpu.SMEM),
            pl.BlockSpec((tt,), lambda i: (i,), memory_space=pltpu.SMEM),
            pl.BlockSpec((tt, PACKED), lambda i: (i, 0)),
            pl.BlockSpec(memory_space=pl.ANY),
        ],
        out_specs=pl.BlockSpec(memory_space=pl.ANY),
        out_shape=jax.ShapeDtypeStruct(xs_init.shape, U32),
        scratch_shapes=[pltpu.SemaphoreType.DMA],
        input_output_aliases={3: 0},
        compiler_params=pltpu.CompilerParams(dimension_semantics=("arbitrary",)),
        name="dispatch",
    )(dest0, dest1, h2p, xs_init)


def _experts_body(ce_ref, row0_ref, nblk_ref, meta_ref, xs_ref, wg_ref, wu_ref, wd_ref, ys_ref,
                  xbuf, obuf, zbuf, wgb_ref, wub_ref, wdb_ref, in_sem, out_sem, z_sem):
    i = pl.program_id(0)
    n_chunks = meta_ref[0]
    slot = lax.rem(i, 2)

    def in_copy(chunk, s):
        r0 = pl.multiple_of(row0_ref[chunk], MOE_BLOCK)
        return pltpu.make_async_copy(xs_ref.at[pl.ds(r0, CHUNK_ROWS)], xbuf.at[s], in_sem.at[s])

    def out_copy(chunk, s, piece):
        r0 = pl.multiple_of(row0_ref[chunk] + piece * MOE_BLOCK, MOE_BLOCK)
        return pltpu.make_async_copy(obuf.at[s, pl.ds(piece * MOE_BLOCK, MOE_BLOCK)],
                                     ys_ref.at[pl.ds(r0, MOE_BLOCK)], out_sem.at[s])

    def for_pieces(chunk, fn):
        for piece in range(CHUNK_BLOCKS):
            @pl.when(piece < nblk_ref[chunk])
            def _():
                fn(piece)

    @pl.when(i == 0)
    def _():
        in_copy(0, 0).start()

    @pl.when(i + 1 < n_chunks)
    def _():
        in_copy(i + 1, 1 - slot).start()

    @pl.when(i < n_chunks)
    def _():
        prev = ce_ref[jnp.maximum(i - 1, 0)]

        @pl.when((i == 0) | (ce_ref[i] != prev))
        def _():
            wgb_ref[...] = wg_ref[...].astype(BF16)
            wub_ref[...] = wu_ref[...].astype(BF16)
            wdb_ref[...] = wd_ref[...].astype(BF16)

        in_copy(i, slot).wait()

        @pl.when(i >= 2)
        def _():
            for_pieces(i - 2, lambda piece: out_copy(i - 2, slot, piece).wait())

        for m in range(1, CHUNK_BLOCKS + 1):
            @pl.when(nblk_ref[i] == m)
            def _():
                rows = m * MOE_BLOCK
                lo, hi = _unpack_bf16_pair(xbuf[slot, 0:rows, :])
                xb = jnp.concatenate([lo.astype(BF16), hi.astype(BF16)], axis=-1)
                g = jnp.dot(xb, wgb_ref[...], preferred_element_type=F32)
                u = jnp.dot(xb, wub_ref[...], preferred_element_type=F32)
                hmid = (g * (1.0 / (1.0 + jnp.exp(-g)))) * u
                y = jnp.dot(hmid.astype(BF16), wdb_ref[...], preferred_element_type=F32)
                obuf[slot, 0:rows, :] = _pack_bf16_pair(y[:, :PACKED], y[:, PACKED:])

        for_pieces(i, lambda piece: out_copy(i, slot, piece).start())

    @pl.when(i == n_chunks - 1)
    def _():
        @pl.when(i >= 1)
        def _():
            for_pieces(i - 1, lambda piece: out_copy(i - 1, 1 - slot, piece).wait())
        for_pieces(i, lambda piece: out_copy(i, slot, piece).wait())

        zero = jnp.zeros(zbuf.shape, F32)
        zbuf[...] = _pack_bf16_pair(zero, zero)
        used = meta_ref[1]
        total = ys_ref.shape[0] // MOE_BLOCK

        def z_copy(blk):
            r0 = pl.multiple_of(blk * MOE_BLOCK, MOE_BLOCK)
            return pltpu.make_async_copy(zbuf, ys_ref.at[pl.ds(r0, MOE_BLOCK)], z_sem)

        def z_start(blk, carry):
            z_copy(blk).start()
            return carry

        def z_wait(blk, carry):
            z_copy(blk).wait()
            return carry

        lax.fori_loop(used, total, z_start, 0)
        lax.fori_loop(used, total, z_wait, 0)


def _experts(chunk_e, chunk_row0, chunk_nblk, meta, xs, n_rows_out, w_gate, w_up, w_down):
    w_map = lambda i, ce, r0, nb, mt: (ce[i], 0, 0)
    return pl.pallas_call(
        _experts_body,
        grid_spec=pltpu.PrefetchScalarGridSpec(
            num_scalar_prefetch=4,
            grid=(chunk_e.shape[0],),
            in_specs=[
                pl.BlockSpec(memory_space=pl.ANY),
                pl.BlockSpec((None, D_MODEL, EXPERT_FF), w_map),
                pl.BlockSpec((None, D_MODEL, EXPERT_FF), w_map),
                pl.BlockSpec((None, EXPERT_FF, D_MODEL), w_map),
            ],
            out_specs=pl.BlockSpec(memory_space=pl.ANY),
            scratch_shapes=[
                pltpu.VMEM((2, CHUNK_ROWS, PACKED), U32),
                pltpu.VMEM((2, CHUNK_ROWS, PACKED), U32),
                pltpu.VMEM((MOE_BLOCK, PACKED), U32),
                pltpu.VMEM((D_MODEL, EXPERT_FF), BF16),
                pltpu.VMEM((D_MODEL, EXPERT_FF), BF16),
                pltpu.VMEM((EXPERT_FF, D_MODEL), BF16),
                pltpu.SemaphoreType.DMA((2,)),
                pltpu.SemaphoreType.DMA((2,)),
                pltpu.SemaphoreType.DMA,
            ],
        ),
        out_shape=jax.ShapeDtypeStruct((n_rows_out, PACKED), U32),
        compiler_params=pltpu.CompilerParams(
            dimension_semantics=("arbitrary",), vmem_limit_bytes=VMEM_LIMIT),
        name="experts",
    )(chunk_e, chunk_row0, chunk_nblk, meta, xs, w_gate, w_up, w_down)


def _combine_body(dest0_ref, dest1_ref, x1_ref, route_ref, fw_ref, ys_ref, out_ref, y0_ref, y1_ref, sem):
    tt = x1_ref.shape[0]

    def issue(t, carry):
        _row_copy(ys_ref, dest0_ref[t], y0_ref, t, sem).start()
        _row_copy(ys_ref, dest1_ref[t], y1_ref, t, sem).start()
        return carry

    lax.fori_loop(0, tt, issue, 0, unroll=8)

    def drain(t, carry):
        _row_copy(ys_ref, 0, y0_ref, 0, sem).wait()
        _row_copy(ys_ref, 0, y1_ref, 0, sem).wait()
        return carry

    lax.fori_loop(0, tt, drain, 0, unroll=8)

    route = route_ref[...]
    g0 = route[:, 2:3]
    g1 = route[:, 3:4]
    a_lo, a_hi = _unpack_bf16_pair(y0_ref[...])
    b_lo, b_hi = _unpack_bf16_pair(y1_ref[...])
    x1 = x1_ref[...]
    z_lo = x1[:, :PACKED] + (a_lo * g0 + b_lo * g1)
    z_hi = x1[:, PACKED:] + (a_hi * g0 + b_hi * g1)
    ms = (jnp.sum(z_lo * z_lo, axis=-1, keepdims=True)
          + jnp.sum(z_hi * z_hi, axis=-1, keepdims=True)) * (1.0 / D_MODEL)
    inv = lax.rsqrt(ms + EPS)
    out_ref[:, :PACKED] = (z_lo * inv) * fw_ref[:, :PACKED]
    out_ref[:, PACKED:] = (z_hi * inv) * fw_ref[:, PACKED:]


def _combine(dest0, dest1, x1, route, fw, ys):
    n, d = x1.shape
    tt = ROW_TILE
    return pl.pallas_call(
        _combine_body,
        grid=(n // tt,),
        in_specs=[
            pl.BlockSpec((tt,), lambda i: (i,), memory_space=pltpu.SMEM),
            pl.BlockSpec((tt,), lambda i: (i,), memory_space=pltpu.SMEM),
            pl.BlockSpec((tt, d), lambda i: (i, 0)),
            pl.BlockSpec((tt, SUBLANES), lambda i: (i, 0)),
            pl.BlockSpec((1, d), lambda i: (0, 0)),
            pl.BlockSpec(memory_space=pl.ANY),
        ],
        out_specs=pl.BlockSpec((tt, d), lambda i: (i, 0)),
        out_shape=jax.ShapeDtypeStruct((n, d), F32),
        scratch_shapes=[
            pltpu.VMEM((tt, PACKED), U32),
            pltpu.VMEM((tt, PACKED), U32),
            pltpu.SemaphoreType.DMA,
        ],
        compiler_params=pltpu.CompilerParams(
            dimension_semantics=("arbitrary",), vmem_limit_bytes=VMEM_LIMIT),
        name="combine",
    )(dest0, dest1, x1, route, fw, ys)


def _retention_tables(seq):
    half = HEAD_DIM // 2
    inv = ROPE_BASE ** (-jnp.arange(half, dtype=F32) / half)
    ang = jnp.arange(seq).astype(F32)[:, None] * inv[None, :]
    cos, sin = jnp.cos(ang), jnp.sin(ang)
    cosf = jnp.concatenate([cos, cos], axis=-1)
    sins = jnp.concatenate([-sin, sin], axis=-1)
    log_g = jnp.log1p(-(2.0 ** (-5.0 - jnp.arange(RET_HEADS, dtype=F32))))
    idx = jnp.arange(CHUNK, dtype=F32)
    rel = idx[:, None] - idx[None, :]
    causal = rel >= 0
    scale = HEAD_DIM ** -0.5
    dmat = jnp.where(causal[None], jnp.exp(log_g[:, None, None] * jnp.where(causal, rel, 0.0)[None]), 0.0)
    k_decay = jnp.exp(log_g[:, None] * (CHUNK - 1 - idx)[None, :])
    q_decay = jnp.exp(log_g[:, None] * (idx + 1)[None, :])
    kdec = jnp.repeat(k_decay.T, HEAD_DIM, axis=1)
    qdec = jnp.repeat(q_decay.T, HEAD_DIM, axis=1) * scale
    return cosf, sins, dmat * scale, kdec, qdec


def _chunk_decay():
    log_g = np.log1p(-(2.0 ** (-5.0 - np.arange(RET_HEADS, dtype=np.float64))))
    return tuple(float(v) for v in np.exp(log_g * CHUNK))


def _layer(x, norm1_w, w_in, ret_gn_w, conv_w, w_o, norm2_w, router_g_w, router_g_b,
           router_e_w, router_e_b, w_gate, w_up, w_down, final_w):
    B, S, D = x.shape
    n = B * S
    cosf, sins, dmat, kdec, qdec = _retention_tables(S)

    wr = jnp.zeros((D, LANES), F32)
    wr = wr.at[:, :N_GROUPS].set(router_g_w)
    wr = wr.at[:, EXPERT_LANE0:EXPERT_LANE0 + N_EXPERTS].set(
        jnp.transpose(router_e_w, (1, 0, 2)).reshape(D, N_EXPERTS))
    wrh = wr.astype(BF16)
    wrl = (wr - wrh.astype(F32)).astype(BF16)
    rb = jnp.zeros((1, LANES), F32)
    rb = rb.at[0, :N_GROUPS].set(router_g_b)
    rb = rb.at[0, EXPERT_LANE0:EXPERT_LANE0 + N_EXPERTS].set(router_e_b.reshape(-1))

    x1, h2p, route, route_t, counts = _mixer_router(
        x, norm1_w[None], w_in.astype(BF16), cosf, sins, dmat, kdec, qdec, ret_gn_w[None], conv_w,
        w_o.astype(BF16), norm2_w[None], wrh, wrl, rb, _chunk_decay())

    expert = route_t[0:2].astype(I32)
    rank = route_t[4:6].astype(I32)
    cnt = counts[0, EXPERT_LANE0:EXPERT_LANE0 + N_EXPERTS].astype(I32)
    padded = ((cnt + MOE_BLOCK - 1) // MOE_BLOCK) * MOE_BLOCK
    pad_ends = jnp.cumsum(padded)
    pad_starts = pad_ends - padded
    eids = jnp.arange(N_EXPERTS, dtype=I32)
    start_of = jnp.sum(jnp.where(expert[:, :, None] == eids, pad_starts, 0), axis=-1)
    dest = start_of + rank
    n_blocks = -(-(n * TOP_K) // MOE_BLOCK) + N_EXPERTS
    used_blocks = pad_ends[-1] // MOE_BLOCK

    n_chunk_slots = -(-(n * TOP_K) // CHUNK_ROWS) + N_EXPERTS
    chunks_of = (padded + CHUNK_ROWS - 1) // CHUNK_ROWS
    chunk_ends = jnp.cumsum(chunks_of)
    n_chunks = chunk_ends[-1]
    ci = jnp.minimum(jnp.arange(n_chunk_slots, dtype=I32), n_chunks - 1)
    chunk_e = jnp.minimum(jnp.sum((chunk_ends[None, :] <= ci[:, None]).astype(I32), axis=1), N_EXPERTS - 1)
    onehot_e = chunk_e[:, None] == eids
    pick = lambda v: jnp.sum(jnp.where(onehot_e, v, 0), axis=-1)
    k = ci - pick(chunk_ends - chunks_of)
    chunk_row0 = pick(pad_starts) + k * CHUNK_ROWS
    chunk_nblk = jnp.clip(pick(padded) // MOE_BLOCK - k * CHUNK_BLOCKS, 0, CHUNK_BLOCKS)
    meta = jnp.stack([n_chunks, used_blocks]).astype(I32)

    xs_rows = n_blocks * MOE_BLOCK + CHUNK_ROWS - MOE_BLOCK
    xs = _dispatch(dest[0], dest[1], h2p.reshape(n, PACKED), jnp.zeros((xs_rows, PACKED), U32))
    ys = _experts(chunk_e, chunk_row0.astype(I32), chunk_nblk.astype(I32), meta, xs,
                  n_blocks * MOE_BLOCK, w_gate, w_up, w_down)
    out = _combine(dest[0], dest[1], x1.reshape(n, D), route.reshape(n, SUBLANES), final_w[None], ys)
    return out.reshape(B, S, D)


def kernel(x, norm1_w, w_in, ret_gn_w, conv_w, w_o, norm2_w, router_g_w, router_g_b, router_e_w,
           router_e_b, w_gate, w_up, w_down, final_norm_w):
    depth = norm1_w.shape[0]
    assert depth == 1, "the final RMSNorm is fused into the last layer's combine step"
    return _layer(x, norm1_w[0], w_in[0], ret_gn_w[0], conv_w[0], w_o[0], norm2_w[0],
                  router_g_w[0], router_g_b[0], router_e_w[0], router_e_b[0],
                  w_gate[0], w_up[0], w_down[0], final_norm_w)
```

```python
import functools

import jax
import jax.numpy as jnp
import numpy as np
from jax import lax
from jax.experimental import pallas as pl
from jax.experimental.pallas import tpu as pltpu

F32 = jnp.float32
BF16 = jnp.bfloat16
U32 = jnp.uint32
I32 = jnp.int32

D_MODEL = 1024
RET_WIDTH = 512
RET_HEADS = 4
HEAD_DIM = 128
CONV_WIDTH = 512
CONV_K = 3
IN_COLS = 4 * RET_WIDTH + 3 * CONV_WIDTH
CHUNK = 128
ROPE_BASE = 10000.0
N_GROUPS = 8
EXPERTS_PER_GROUP = 8
N_EXPERTS = 64
TOP_K = 2
EXPERT_FF = 512
MOE_BLOCK = 128
EPS = 1e-6

LANES = 128
SUBLANES = 8
PACKED = D_MODEL // 2
EXPERT_LANE0 = N_GROUPS
VMEM_LIMIT = 48 * 1024 * 1024

MIX_TILE = 256
ROW_TILE = 1024
PROJ_PIECE = 512
PIECES_IN_RETENTION = 5
PIECES_BEFORE_CONV = 0
CHUNK_BLOCKS = 4
CHUNK_ROWS = CHUNK_BLOCKS * MOE_BLOCK


def _pack_bf16_pair(lo, hi):
    return pltpu.pack_elementwise([lo, hi], packed_dtype=BF16)


def _unpack_bf16_pair(packed):
    lo = pltpu.unpack_elementwise(packed, index=0, packed_dtype=BF16, unpacked_dtype=F32)
    hi = pltpu.unpack_elementwise(packed, index=1, packed_dtype=BF16, unpacked_dtype=F32)
    return lo, hi


def _rms(x, w):
    ms = jnp.mean(x * x, axis=-1, keepdims=True)
    return (x * lax.rsqrt(ms + EPS)) * w


def _in_projection(x_ref, n1_ref, win_ref, proj_ref):
    h = _rms(x_ref[...], n1_ref[...])
    proj_ref[...] = jnp.dot(h.astype(BF16), win_ref[...], preferred_element_type=F32)


def _mixer_router_body(x_ref, xn_ref, n1_ref, win_ref, cos_ref, sin_ref, dmat_ref, kdec_ref, qdec_ref,
                       gnw_ref, convw_ref, wo_ref, n2_ref, wrh_ref, wrl_ref, rb_ref,
                       x1_ref, h2p_ref, route_ref, route_t_ref, cnt_ref,
                       state_ref, cu_ref, mix_ref, run_ref, proj_a_ref, proj_b_ref, hn_ref,
                       *, chunk_decay, tiles_per_seq):
    i = pl.program_id(0)
    c = lax.rem(i, tiles_per_seq)

    @pl.when(c == 0)
    def _():
        state_ref[...] = jnp.zeros_like(state_ref)
        cu_ref[0:SUBLANES, :] = jnp.zeros((SUBLANES, CONV_WIDTH), F32)

    @pl.when(i == 0)
    def _():
        run_ref[...] = jnp.zeros_like(run_ref)
        _in_projection(x_ref, n1_ref, win_ref, proj_a_ref)

    def step(proj_ref, proj_next_ref):
        hn_ref[...] = _rms(xn_ref[...], n1_ref[...]).astype(BF16)

        def project_piece(p):
            cols = slice(p * PROJ_PIECE, (p + 1) * PROJ_PIECE)
            proj_next_ref[:, cols] = jnp.dot(hn_ref[...], win_ref[:, cols], preferred_element_type=F32)

        _mix_and_route(proj_ref, x_ref, cos_ref, sin_ref, dmat_ref, kdec_ref, qdec_ref, gnw_ref,
                       convw_ref, wo_ref, n2_ref, wrh_ref, wrl_ref, rb_ref,
                       x1_ref, h2p_ref, route_ref, route_t_ref, cnt_ref,
                       state_ref, cu_ref, mix_ref, run_ref, chunk_decay, project_piece)

    @pl.when(lax.rem(i, 2) == 0)
    def _():
        step(proj_a_ref, proj_b_ref)

    @pl.when(lax.rem(i, 2) == 1)
    def _():
        step(proj_b_ref, proj_a_ref)


def _mix_and_route(proj, x_ref, cos_ref, sin_ref, dmat_ref, kdec_ref, qdec_ref, gnw_ref,
                   convw_ref, wo_ref, n2_ref, wrh_ref, wrl_ref, rb_ref,
                   x1_ref, h2p_ref, route_ref, route_t_ref, cnt_ref,
                   state_ref, cu_ref, mix_ref, run_ref, chunk_decay, project_piece):
    tm = x_ref.shape[0]
    x = x_ref[...]
    R = RET_WIDTH
    cosf = cos_ref[...]
    sins = sin_ref[...]
    n_pieces = IN_COLS // PROJ_PIECE
    piece = 0

    for j in range(tm // CHUNK):
        r0 = j * CHUNK
        for hh in range(RET_HEADS):
            if piece < PIECES_IN_RETENTION:
                project_piece(piece)
                piece += 1
            c0 = hh * HEAD_DIM
            qh = proj[r0:r0 + CHUNK, c0:c0 + HEAD_DIM]
            kh = proj[r0:r0 + CHUNK, R + c0:R + c0 + HEAD_DIM]
            vh = proj[r0:r0 + CHUNK, 2 * R + c0:2 * R + c0 + HEAD_DIM]
            gh = proj[r0:r0 + CHUNK, 3 * R + c0:3 * R + c0 + HEAD_DIM]
            cs = cosf[r0:r0 + CHUNK, :]
            sn = sins[r0:r0 + CHUNK, :]
            qr = qh * cs + pltpu.roll(qh, HEAD_DIM // 2, 1) * sn
            kr = kh * cs + pltpu.roll(kh, HEAD_DIM // 2, 1) * sn
            qb = qr.astype(BF16)
            kb = kr.astype(BF16)
            vb = vh.astype(BF16)
            scores = lax.dot_general(qb, kb, (((1,), (1,)), ((), ())), preferred_element_type=F32)
            scores = scores * dmat_ref[hh]
            o = jnp.dot(scores.astype(BF16), vb, preferred_element_type=F32)
            s_prev = state_ref[hh]
            qd = (qr * qdec_ref[:, c0:c0 + HEAD_DIM]).astype(BF16)
            o = o + jnp.dot(qd, s_prev.astype(BF16), preferred_element_type=F32)
            kd = (kr * kdec_ref[:, c0:c0 + HEAD_DIM]).astype(BF16)
            kv = lax.dot_general(kd, vb, (((0,), (0,)), ((), ())), preferred_element_type=F32)
            state_ref[hh] = chunk_decay[hh] * s_prev + kv
            mu = jnp.mean(o, axis=-1, keepdims=True)
            d = o - mu
            var = jnp.mean(d * d, axis=-1, keepdims=True)
            y = d * lax.rsqrt(var + EPS) * gnw_ref[:, c0:c0 + HEAD_DIM]
            gate = gh * (1.0 / (1.0 + jnp.exp(-gh)))
            mix_ref[r0:r0 + CHUNK, c0:c0 + HEAD_DIM] = (gate * y).astype(BF16)

    while piece < PIECES_IN_RETENTION + PIECES_BEFORE_CONV:
        project_piece(piece)
        piece += 1

    gb = proj[:, 4 * R:4 * R + CONV_WIDTH]
    gc = proj[:, 4 * R + CONV_WIDTH:4 * R + 2 * CONV_WIDTH]
    u = proj[:, 4 * R + 2 * CONV_WIDTH:4 * R + 3 * CONV_WIDTH]
    cu_ref[SUBLANES:SUBLANES + tm, :] = gc * u
    conv = (convw_ref[2:3, :] * cu_ref[SUBLANES:SUBLANES + tm, :]
            + convw_ref[1:2, :] * cu_ref[SUBLANES - 1:SUBLANES - 1 + tm, :]
            + convw_ref[0:1, :] * cu_ref[SUBLANES - 2:SUBLANES - 2 + tm, :])
    mix_ref[:, R:R + CONV_WIDTH] = (gb * conv).astype(BF16)
    cu_ref[0:SUBLANES, :] = cu_ref[tm:tm + SUBLANES, :]

    x1 = x + jnp.dot(mix_ref[...], wo_ref[...], preferred_element_type=F32)
    x1_ref[...] = x1
    h2 = _rms(x1, n2_ref[...])
    h2p_ref[...] = _pack_bf16_pair(h2[:, :PACKED], h2[:, PACKED:])

    hi = h2.astype(BF16)
    lo = (h2 - hi.astype(F32)).astype(BF16)
    wrh = wrh_ref[...]
    logits = (jnp.dot(hi, wrh, preferred_element_type=F32)
              + jnp.dot(lo, wrh, preferred_element_type=F32)
              + jnp.dot(hi, wrl_ref[...], preferred_element_type=F32)
              + rb_ref[...])

    while piece < n_pieces:
        project_piece(piece)
        piece += 1

    lane = lax.broadcasted_iota(I32, (tm, LANES), 1)
    lane_f = lane.astype(F32)
    neg_inf = jnp.float32(-jnp.inf)

    def first_max(v):
        m = jnp.max(v, axis=-1, keepdims=True)
        return m, jnp.min(jnp.where(v == m, lane_f, float(LANES)), axis=-1, keepdims=True)

    gl = jnp.where(lane < N_GROUPS, logits, neg_inf)
    gmax, g_sel = first_max(gl)
    p_group = 1.0 / jnp.sum(jnp.exp(gl - gmax), axis=-1, keepdims=True)

    e_lo = EXPERT_LANE0 + g_sel * EXPERTS_PER_GROUP
    emask = (lane_f >= e_lo) & (lane_f < e_lo + EXPERTS_PER_GROUP)
    el = jnp.where(emask, logits, neg_inf)
    l1, i1 = first_max(el)
    l2, i2 = first_max(jnp.where(lane_f == i1, neg_inf, el))
    r = jnp.exp(l2 - l1)
    gate1 = p_group / (1.0 + r)
    gate2 = gate1 * r

    oh1 = jnp.where(lane_f == i1, 1.0, 0.0)
    oh2 = jnp.where(lane_f == i2, 1.0, 0.0)
    row = lax.broadcasted_iota(I32, (tm, tm), 0)
    col = lax.broadcasted_iota(I32, (tm, tm), 1)
    lower = jnp.where(row > col, 1.0, 0.0).astype(BF16)
    pre1 = jnp.dot(lower, oh1.astype(BF16), preferred_element_type=F32)
    pre2 = jnp.dot(lower, oh2.astype(BF16), preferred_element_type=F32)
    cnt1 = jnp.sum(oh1, axis=0, keepdims=True)
    cnt2 = jnp.sum(oh2, axis=0, keepdims=True)
    run = run_ref[...]
    rank1 = jnp.sum(oh1 * (pre1 + run), axis=-1, keepdims=True)
    rank2 = jnp.sum(oh2 * (pre2 + run + cnt1), axis=-1, keepdims=True)
    run_new = run + cnt1 + cnt2
    run_ref[...] = run_new
    cnt_ref[...] = run_new

    route = jnp.where(lane == 0, i1 - EXPERT_LANE0,
            jnp.where(lane == 1, i2 - EXPERT_LANE0,
            jnp.where(lane == 2, gate1,
            jnp.where(lane == 3, gate2,
            jnp.where(lane == 4, rank1,
            jnp.where(lane == 5, rank2, 0.0))))))
    route_ref[...] = route[:, :SUBLANES]
    route_t_ref[...] = jnp.transpose(route)[:SUBLANES, :]


def _mixer_router(x, n1, win_bf, cosf, sins, dmat, kdec, qdec, gnw, convw, wo_bf, n2, wrh, wrl, rb,
                  chunk_decay):
    B, S, D = x.shape
    tm = MIX_TILE
    n = B * S
    tiles_per_seq = S // tm
    n_tiles = n // tm
    x = x.reshape(n, D)
    const2 = lambda i: (0, 0)
    tile = lambda i: (i, 0)
    in_specs = [
        pl.BlockSpec((tm, D), tile),
        pl.BlockSpec((tm, D), lambda i: (jnp.minimum(i + 1, n_tiles - 1), 0)),
        pl.BlockSpec((1, D), const2),
        pl.BlockSpec((D, IN_COLS), const2),
        pl.BlockSpec((tm, HEAD_DIM), lambda i: (lax.rem(i, tiles_per_seq), 0)),
        pl.BlockSpec((tm, HEAD_DIM), lambda i: (lax.rem(i, tiles_per_seq), 0)),
        pl.BlockSpec((RET_HEADS, CHUNK, CHUNK), lambda i: (0, 0, 0)),
        pl.BlockSpec((CHUNK, RET_WIDTH), const2),
        pl.BlockSpec((CHUNK, RET_WIDTH), const2),
        pl.BlockSpec((1, RET_WIDTH), const2),
        pl.BlockSpec((CONV_K, CONV_WIDTH), const2),
        pl.BlockSpec((D, D), const2),
        pl.BlockSpec((1, D), const2),
        pl.BlockSpec((D, LANES), const2),
        pl.BlockSpec((D, LANES), const2),
        pl.BlockSpec((1, LANES), const2),
    ]
    out_specs = [
        pl.BlockSpec((tm, D), tile),
        pl.BlockSpec((tm, PACKED), tile),
        pl.BlockSpec((tm, SUBLANES), tile),
        pl.BlockSpec((SUBLANES, tm), lambda i: (0, i)),
        pl.BlockSpec((1, LANES), const2),
    ]
    out_shape = [
        jax.ShapeDtypeStruct((n, D), F32),
        jax.ShapeDtypeStruct((n, PACKED), U32),
        jax.ShapeDtypeStruct((n, SUBLANES), F32),
        jax.ShapeDtypeStruct((SUBLANES, n), F32),
        jax.ShapeDtypeStruct((1, LANES), F32),
    ]
    scratch = [
        pltpu.VMEM((RET_HEADS, HEAD_DIM, HEAD_DIM), F32),
        pltpu.VMEM((tm + SUBLANES, CONV_WIDTH), F32),
        pltpu.VMEM((tm, D), BF16),
        pltpu.VMEM((1, LANES), F32),
        pltpu.VMEM((tm, IN_COLS), F32),
        pltpu.VMEM((tm, IN_COLS), F32),
        pltpu.VMEM((tm, D), BF16),
    ]
    return pl.pallas_call(
        functools.partial(_mixer_router_body, chunk_decay=chunk_decay, tiles_per_seq=tiles_per_seq),
        grid=(n_tiles,), in_specs=in_specs, out_specs=out_specs, out_shape=out_shape,
        scratch_shapes=scratch,
        compiler_params=pltpu.CompilerParams(
            dimension_semantics=("arbitrary",), vmem_limit_bytes=VMEM_LIMIT),
        name="mixer_router",
    )(x, x, n1, win_bf, cosf, sins, dmat, kdec, qdec, gnw, convw, wo_bf, n2, wrh, wrl, rb)


def _row_copy(src_ref, src_row, dst_ref, dst_row, sem):
    return pltpu.make_async_copy(src_ref.at[pl.ds(src_row, 1)], dst_ref.at[pl.ds(dst_row, 1)], sem)


def _dispatch_body(dest0_ref, dest1_ref, h2p_ref, xs_in_ref, xs_ref, sem):
    del xs_in_ref
    tt = h2p_ref.shape[0]

    def issue(t, carry):
        _row_copy(h2p_ref, t, xs_ref, dest0_ref[t], sem).start(priority=0)
        _row_copy(h2p_ref, t, xs_ref, dest1_ref[t], sem).start(priority=1)
        return carry

    lax.fori_loop(0, tt, issue, 0, unroll=8)

    def drain(t, carry):
        _row_copy(h2p_ref, 0, xs_ref, 0, sem).wait()
        _row_copy(h2p_ref, 0, xs_ref, 0, sem).wait()
        return carry

    lax.fori_loop(0, tt, drain, 0, unroll=8)


def _dispatch(dest0, dest1, h2p, xs_init):
    n = h2p.shape[0]
    tt = ROW_TILE
    return pl.pallas_call(
        _dispatch_body,
        grid=(n // tt,),
        in_specs=[
            pl.BlockSpec((tt,), lambda i: (i,), memory_space=pltpu.SMEM),
            pl.BlockSpec((tt,), lambda i: (i,), memory_space=pltpu.SMEM),
            pl.BlockSpec((tt, PACKED), lambda i: (i, 0)),
            pl.BlockSpec(memory_space=pl.ANY),
        ],
        out_specs=pl.BlockSpec(memory_space=pl.ANY),
        out_shape=jax.ShapeDtypeStruct(xs_init.shape, U32),
        scratch_shapes=[pltpu.SemaphoreType.DMA],
        input_output_aliases={3: 0},
        compiler_params=pltpu.CompilerParams(dimension_semantics=("arbitrary",)),
        name="dispatch",
    )(dest0, dest1, h2p, xs_init)


def _experts_body(ce_ref, row0_ref, nblk_ref, meta_ref, xs_ref, wg_ref, wu_ref, wd_ref, ys_ref,
                  xbuf, obuf, zbuf, wgb_ref, wub_ref, wdb_ref, in_sem, out_sem, z_sem):
    i = pl.program_id(0)
    n_chunks = meta_ref[0]
    slot = lax.rem(i, 2)

    def in_copy(chunk, s):
        r0 = pl.multiple_of(row0_ref[chunk], MOE_BLOCK)
        return pltpu.make_async_copy(xs_ref.at[pl.ds(r0, CHUNK_ROWS)], xbuf.at[s], in_sem.at[s])

    def out_copy(chunk, s, piece):
        r0 = pl.multiple_of(row0_ref[chunk] + piece * MOE_BLOCK, MOE_BLOCK)
        return pltpu.make_async_copy(obuf.at[s, pl.ds(piece * MOE_BLOCK, MOE_BLOCK)],
                                     ys_ref.at[pl.ds(r0, MOE_BLOCK)], out_sem.at[s])

    def for_pieces(chunk, fn):
        for piece in range(CHUNK_BLOCKS):
            @pl.when(piece < nblk_ref[chunk])
            def _():
                fn(piece)

    @pl.when(i == 0)
    def _():
        in_copy(0, 0).start()

    @pl.when(i + 1 < n_chunks)
    def _():
        in_copy(i + 1, 1 - slot).start()

    @pl.when(i < n_chunks)
    def _():
        prev = ce_ref[jnp.maximum(i - 1, 0)]

        @pl.when((i == 0) | (ce_ref[i] != prev))
        def _():
            wgb_ref[...] = wg_ref[...].astype(BF16)
            wub_ref[...] = wu_ref[...].astype(BF16)
            wdb_ref[...] = wd_ref[...].astype(BF16)

        in_copy(i, slot).wait()

        @pl.when(i >= 2)
        def _():
            for_pieces(i - 2, lambda piece: out_copy(i - 2, slot, piece).wait())

        for m in range(1, CHUNK_BLOCKS + 1):
            @pl.when(nblk_ref[i] == m)
            def _():
                rows = m * MOE_BLOCK
                lo, hi = _unpack_bf16_pair(xbuf[slot, 0:rows, :])
                xb = jnp.concatenate([lo.astype(BF16), hi.astype(BF16)], axis=-1)
                g = jnp.dot(xb, wgb_ref[...], preferred_element_type=F32)
                u = jnp.dot(xb, wub_ref[...], preferred_element_type=F32)
                hmid = (g * (1.0 / (1.0 + jnp.exp(-g)))) * u
                y = jnp.dot(hmid.astype(BF16), wdb_ref[...], preferred_element_type=F32)
                obuf[slot, 0:rows, :] = _pack_bf16_pair(y[:, :PACKED], y[:, PACKED:])

        for_pieces(i, lambda piece: out_copy(i, slot, piece).start())

    @pl.when(i == n_chunks - 1)
    def _():
        @pl.when(i >= 1)
        def _():
            for_pieces(i - 1, lambda piece: out_copy(i - 1, 1 - slot, piece).wait())
        for_pieces(i, lambda piece: out_copy(i, slot, piece).wait())

        zero = jnp.zeros(zbuf.shape, F32)
        zbuf[...] = _pack_bf16_pair(zero, zero)
        used = meta_ref[1]
        total = ys_ref.shape[0] // MOE_BLOCK

        def z_copy(blk):
            r0 = pl.multiple_of(blk * MOE_BLOCK, MOE_BLOCK)
            return pltpu.make_async_copy(zbuf, ys_ref.at[pl.ds(r0, MOE_BLOCK)], z_sem)

        def z_start(blk, carry):
            z_copy(blk).start()
            return carry

        def z_wait(blk, carry):
            z_copy(blk).wait()
            return carry

        lax.fori_loop(used, total, z_start, 0)
        lax.fori_loop(used, total, z_wait, 0)


def _experts(chunk_e, chunk_row0, chunk_nblk, meta, xs, n_rows_out, w_gate, w_up, w_down):
    w_map = lambda i, ce, r0, nb, mt: (ce[i], 0, 0)
    return pl.pallas_call(
        _experts_body,
        grid_spec=pltpu.PrefetchScalarGridSpec(
            num_scalar_prefetch=4,
            grid=(chunk_e.shape[0],),
            in_specs=[
                pl.BlockSpec(memory_space=pl.ANY),
                pl.BlockSpec((None, D_MODEL, EXPERT_FF), w_map),
                pl.BlockSpec((None, D_MODEL, EXPERT_FF), w_map),
                pl.BlockSpec((None, EXPERT_FF, D_MODEL), w_map),
            ],
            out_specs=pl.BlockSpec(memory_space=pl.ANY),
            scratch_shapes=[
                pltpu.VMEM((2, CHUNK_ROWS, PACKED), U32),
                pltpu.VMEM((2, CHUNK_ROWS, PACKED), U32),
                pltpu.VMEM((MOE_BLOCK, PACKED), U32),
                pltpu.VMEM((D_MODEL, EXPERT_FF), BF16),
                pltpu.VMEM((D_MODEL, EXPERT_FF), BF16),
                pltpu.VMEM((EXPERT_FF, D_MODEL), BF16),
                pltpu.SemaphoreType.DMA((2,)),
                pltpu.SemaphoreType.DMA((2,)),
                pltpu.SemaphoreType.DMA,
            ],
        ),
        out_shape=jax.ShapeDtypeStruct((n_rows_out, PACKED), U32),
        compiler_params=pltpu.CompilerParams(
            dimension_semantics=("arbitrary",), vmem_limit_bytes=VMEM_LIMIT),
        name="experts",
    )(chunk_e, chunk_row0, chunk_nblk, meta, xs, w_gate, w_up, w_down)


def _combine_body(dest0_ref, dest1_ref, x1_ref, route_ref, fw_ref, ys_ref, out_ref, y0_ref, y1_ref, sem):
    tt = x1_ref.shape[0]

    def issue(t, carry):
        _row_copy(ys_ref, dest0_ref[t], y0_ref, t, sem).start(priority=0)
        _row_copy(ys_ref, dest1_ref[t], y1_ref, t, sem).start(priority=1)
        return carry

    lax.fori_loop(0, tt, issue, 0, unroll=8)

    def drain(t, carry):
        _row_copy(ys_ref, 0, y0_ref, 0, sem).wait()
        _row_copy(ys_ref, 0, y1_ref, 0, sem).wait()
        return carry

    lax.fori_loop(0, tt, drain, 0, unroll=8)

    route = route_ref[...]
    g0 = route[:, 2:3]
    g1 = route[:, 3:4]
    a_lo, a_hi = _unpack_bf16_pair(y0_ref[...])
    b_lo, b_hi = _unpack_bf16_pair(y1_ref[...])
    x1 = x1_ref[...]
    z_lo = x1[:, :PACKED] + (a_lo * g0 + b_lo * g1)
    z_hi = x1[:, PACKED:] + (a_hi * g0 + b_hi * g1)
    ms = (jnp.sum(z_lo * z_lo, axis=-1, keepdims=True)
          + jnp.sum(z_hi * z_hi, axis=-1, keepdims=True)) * (1.0 / D_MODEL)
    inv = lax.rsqrt(ms + EPS)
    out_ref[:, :PACKED] = (z_lo * inv) * fw_ref[:, :PACKED]
    out_ref[:, PACKED:] = (z_hi * inv) * fw_ref[:, PACKED:]


def _combine(dest0, dest1, x1, route, fw, ys):
    n, d = x1.shape
    tt = ROW_TILE
    return pl.pallas_call(
        _combine_body,
        grid=(n // tt,),
        in_specs=[
            pl.BlockSpec((tt,), lambda i: (i,), memory_space=pltpu.SMEM),
            pl.BlockSpec((tt,), lambda i: (i,), memory_space=pltpu.SMEM),
            pl.BlockSpec((tt, d), lambda i: (i, 0)),
            pl.BlockSpec((tt, SUBLANES), lambda i: (i, 0)),
            pl.BlockSpec((1, d), lambda i: (0, 0)),
            pl.BlockSpec(memory_space=pl.ANY),
        ],
        out_specs=pl.BlockSpec((tt, d), lambda i: (i, 0)),
        out_shape=jax.ShapeDtypeStruct((n, d), F32),
        scratch_shapes=[
            pltpu.VMEM((tt, PACKED), U32),
            pltpu.VMEM((tt, PACKED), U32),
            pltpu.SemaphoreType.DMA,
        ],
        compiler_params=pltpu.CompilerParams(
            dimension_semantics=("arbitrary",), vmem_limit_bytes=VMEM_LIMIT),
        name="combine",
    )(dest0, dest1, x1, route, fw, ys)


def _retention_tables(seq):
    half = HEAD_DIM // 2
    inv = ROPE_BASE ** (-jnp.arange(half, dtype=F32) / half)
    ang = jnp.arange(seq).astype(F32)[:, None] * inv[None, :]
    cos, sin = jnp.cos(ang), jnp.sin(ang)
    cosf = jnp.concatenate([cos, cos], axis=-1)
    sins = jnp.concatenate([-sin, sin], axis=-1)
    log_g = jnp.log1p(-(2.0 ** (-5.0 - jnp.arange(RET_HEADS, dtype=F32))))
    idx = jnp.arange(CHUNK, dtype=F32)
    rel = idx[:, None] - idx[None, :]
    causal = rel >= 0
    scale = HEAD_DIM ** -0.5
    dmat = jnp.where(causal[None], jnp.exp(log_g[:, None, None] * jnp.where(causal, rel, 0.0)[None]), 0.0)
    k_decay = jnp.exp(log_g[:, None] * (CHUNK - 1 - idx)[None, :])
    q_decay = jnp.exp(log_g[:, None] * (idx + 1)[None, :])
    kdec = jnp.repeat(k_decay.T, HEAD_DIM, axis=1)
    qdec = jnp.repeat(q_decay.T, HEAD_DIM, axis=1) * scale
    return cosf, sins, dmat * scale, kdec, qdec


def _chunk_decay():
    log_g = np.log1p(-(2.0 ** (-5.0 - np.arange(RET_HEADS, dtype=np.float64))))
    return tuple(float(v) for v in np.exp(log_g * CHUNK))


def _layer(x, norm1_w, w_in, ret_gn_w, conv_w, w_o, norm2_w, router_g_w, router_g_b,
           router_e_w, router_e_b, w_gate, w_up, w_down, final_w):
    B, S, D = x.shape
    n = B * S
    cosf, sins, dmat, kdec, qdec = _retention_tables(S)

    wr = jnp.zeros((D, LANES), F32)
    wr = wr.at[:, :N_GROUPS].set(router_g_w)
    wr = wr.at[:, EXPERT_LANE0:EXPERT_LANE0 + N_EXPERTS].set(
        jnp.transpose(router_e_w, (1, 0, 2)).reshape(D, N_EXPERTS))
    wrh = wr.astype(BF16)
    wrl = (wr - wrh.astype(F32)).astype(BF16)
    rb = jnp.zeros((1, LANES), F32)
    rb = rb.at[0, :N_GROUPS].set(router_g_b)
    rb = rb.at[0, EXPERT_LANE0:EXPERT_LANE0 + N_EXPERTS].set(router_e_b.reshape(-1))

    x1, h2p, route, route_t, counts = _mixer_router(
        x, norm1_w[None], w_in.astype(BF16), cosf, sins, dmat, kdec, qdec, ret_gn_w[None], conv_w,
        w_o.astype(BF16), norm2_w[None], wrh, wrl, rb, _chunk_decay())

    expert = route_t[0:2].astype(I32)
    rank = route_t[4:6].astype(I32)
    cnt = counts[0, EXPERT_LANE0:EXPERT_LANE0 + N_EXPERTS].astype(I32)
    padded = ((cnt + MOE_BLOCK - 1) // MOE_BLOCK) * MOE_BLOCK
    pad_ends = jnp.cumsum(padded)
    pad_starts = pad_ends - padded
    eids = jnp.arange(N_EXPERTS, dtype=I32)
    start_of = jnp.sum(jnp.where(expert[:, :, None] == eids, pad_starts, 0), axis=-1)
    dest = start_of + rank
    n_blocks = -(-(n * TOP_K) // MOE_BLOCK) + N_EXPERTS
    used_blocks = pad_ends[-1] // MOE_BLOCK

    n_chunk_slots = -(-(n * TOP_K) // CHUNK_ROWS) + N_EXPERTS
    chunks_of = (padded + CHUNK_ROWS - 1) // CHUNK_ROWS
    chunk_ends = jnp.cumsum(chunks_of)
    n_chunks = chunk_ends[-1]
    ci = jnp.minimum(jnp.arange(n_chunk_slots, dtype=I32), n_chunks - 1)
    chunk_e = jnp.minimum(jnp.sum((chunk_ends[None, :] <= ci[:, None]).astype(I32), axis=1), N_EXPERTS - 1)
    onehot_e = chunk_e[:, None] == eids
    pick = lambda v: jnp.sum(jnp.where(onehot_e, v, 0), axis=-1)
    k = ci - pick(chunk_ends - chunks_of)
    chunk_row0 = pick(pad_starts) + k * CHUNK_ROWS
    chunk_nblk = jnp.clip(pick(padded) // MOE_BLOCK - k * CHUNK_BLOCKS, 0, CHUNK_BLOCKS)
    meta = jnp.stack([n_chunks, used_blocks]).astype(I32)

    xs_rows = n_blocks * MOE_BLOCK + CHUNK_ROWS - MOE_BLOCK
    xs = _dispatch(dest[0], dest[1], h2p.reshape(n, PACKED), jnp.zeros((xs_rows, PACKED), U32))
    ys = _experts(chunk_e, chunk_row0.astype(I32), chunk_nblk.astype(I32), meta, xs,
                  n_blocks * MOE_BLOCK, w_gate, w_up, w_down)
    out = _combine(dest[0], dest[1], x1.reshape(n, D), route.reshape(n, SUBLANES), final_w[None], ys)
    return out.reshape(B, S, D)


def kernel(x, norm1_w, w_in, ret_gn_w, conv_w, w_o, norm2_w, router_g_w, router_g_b, router_e_w,
           router_e_b, w_gate, w_up, w_down, final_norm_w):
    depth = norm1_w.shape[0]
    assert depth == 1, "the final RMSNorm is fused into the last layer's combine step"
    return _layer(x, norm1_w[0], w_in[0], ret_gn_w[0], conv_w[0], w_o[0], norm2_w[0],
                  router_g_w[0], router_g_b[0], router_e_w[0], router_e_b[0],
                  w_gate[0], w_up[0], w_down[0], final_norm_w)
```

```python
import functools

import jax
import jax.numpy as jnp
import numpy as np
from jax import lax
from jax.experimental import pallas as pl
from jax.experimental.pallas import tpu as pltpu

F32 = jnp.float32
BF16 = jnp.bfloat16
U32 = jnp.uint32
I32 = jnp.int32

D_MODEL = 1024
RET_WIDTH = 512
RET_HEADS = 4
HEAD_DIM = 128
CONV_WIDTH = 512
CONV_K = 3
IN_COLS = 4 * RET_WIDTH + 3 * CONV_WIDTH
CHUNK = 128
ROPE_BASE = 10000.0
N_GROUPS = 8
EXPERTS_PER_GROUP = 8
N_EXPERTS = 64
TOP_K = 2
EXPERT_FF = 512
MOE_BLOCK = 128
EPS = 1e-6

LANES = 128
SUBLANES = 8
PACKED = D_MODEL // 2
EXPERT_LANE0 = N_GROUPS
VMEM_LIMIT = 48 * 1024 * 1024

MIX_TILE = 256
ROW_TILE = 1024
PROJ_PIECE = 512
PIECES_IN_RETENTION = 5
PIECES_BEFORE_CONV = 0
CHUNK_BLOCKS = 4
CHUNK_ROWS = CHUNK_BLOCKS * MOE_BLOCK


def _pack_bf16_pair(lo, hi):
    return pltpu.pack_elementwise([lo, hi], packed_dtype=BF16)


def _unpack_bf16_pair(packed):
    lo = pltpu.unpack_elementwise(packed, index=0, packed_dtype=BF16, unpacked_dtype=F32)
    hi = pltpu.unpack_elementwise(packed, index=1, packed_dtype=BF16, unpacked_dtype=F32)
    return lo, hi


def _rms(x, w):
    ms = jnp.mean(x * x, axis=-1, keepdims=True)
    return (x * lax.rsqrt(ms + EPS)) * w


def _in_projection(x_ref, n1_ref, win_ref, proj_ref):
    h = _rms(x_ref[...], n1_ref[...])
    proj_ref[...] = jnp.dot(h.astype(BF16), win_ref[...], preferred_element_type=F32)


def _mixer_router_body(x_ref, xn_ref, n1_ref, win_ref, cos_ref, sin_ref, dmat_ref, kdec_ref, qdec_ref,
                       gnw_ref, convw_ref, wo_ref, n2_ref, wrh_ref, wrl_ref, rb_ref,
                       x1_ref, h2p_ref, route_ref, route_t_ref, cnt_ref,
                       state_ref, cu_ref, mix_ref, run_ref, proj_a_ref, proj_b_ref, hn_ref,
                       *, chunk_decay, tiles_per_seq):
    i = pl.program_id(0)
    c = lax.rem(i, tiles_per_seq)

    @pl.when(c == 0)
    def _():
        state_ref[...] = jnp.zeros_like(state_ref)
        cu_ref[0:SUBLANES, :] = jnp.zeros((SUBLANES, CONV_WIDTH), F32)

    @pl.when(i == 0)
    def _():
        run_ref[...] = jnp.zeros_like(run_ref)
        _in_projection(x_ref, n1_ref, win_ref, proj_a_ref)

    def step(proj_ref, proj_next_ref):
        hn_ref[...] = _rms(xn_ref[...], n1_ref[...]).astype(BF16)

        def project_piece(p):
            cols = slice(p * PROJ_PIECE, (p + 1) * PROJ_PIECE)
            proj_next_ref[:, cols] = jnp.dot(hn_ref[...], win_ref[:, cols], preferred_element_type=F32)

        _mix_and_route(proj_ref, x_ref, cos_ref, sin_ref, dmat_ref, kdec_ref, qdec_ref, gnw_ref,
                       convw_ref, wo_ref, n2_ref, wrh_ref, wrl_ref, rb_ref,
                       x1_ref, h2p_ref, route_ref, route_t_ref, cnt_ref,
                       state_ref, cu_ref, mix_ref, run_ref, chunk_decay, project_piece)

    @pl.when(lax.rem(i, 2) == 0)
    def _():
        step(proj_a_ref, proj_b_ref)

    @pl.when(lax.rem(i, 2) == 1)
    def _():
        step(proj_b_ref, proj_a_ref)


def _mix_and_route(proj, x_ref, cos_ref, sin_ref, dmat_ref, kdec_ref, qdec_ref, gnw_ref,
                   convw_ref, wo_ref, n2_ref, wrh_ref, wrl_ref, rb_ref,
                   x1_ref, h2p_ref, route_ref, route_t_ref, cnt_ref,
                   state_ref, cu_ref, mix_ref, run_ref, chunk_decay, project_piece):
    tm = x_ref.shape[0]
    x = x_ref[...]
    R = RET_WIDTH
    cosf = cos_ref[...]
    sins = sin_ref[...]
    n_pieces = IN_COLS // PROJ_PIECE
    piece = 0

    for j in range(tm // CHUNK):
        r0 = j * CHUNK
        for hh in range(RET_HEADS):
            if piece < PIECES_IN_RETENTION:
                project_piece(piece)
                piece += 1
            c0 = hh * HEAD_DIM
            qh = proj[r0:r0 + CHUNK, c0:c0 + HEAD_DIM]
            kh = proj[r0:r0 + CHUNK, R + c0:R + c0 + HEAD_DIM]
            vh = proj[r0:r0 + CHUNK, 2 * R + c0:2 * R + c0 + HEAD_DIM]
            gh = proj[r0:r0 + CHUNK, 3 * R + c0:3 * R + c0 + HEAD_DIM]
            cs = cosf[r0:r0 + CHUNK, :]
            sn = sins[r0:r0 + CHUNK, :]
            qr = qh * cs + pltpu.roll(qh, HEAD_DIM // 2, 1) * sn
            kr = kh * cs + pltpu.roll(kh, HEAD_DIM // 2, 1) * sn
            qb = qr.astype(BF16)
            kb = kr.astype(BF16)
            vb = vh.astype(BF16)
            scores = lax.dot_general(qb, kb, (((1,), (1,)), ((), ())), preferred_element_type=F32)
            scores = scores * dmat_ref[hh]
            o = jnp.dot(scores.astype(BF16), vb, preferred_element_type=F32)
            s_prev = state_ref[hh]
            qd = (qr * qdec_ref[:, c0:c0 + HEAD_DIM]).astype(BF16)
            o = o + jnp.dot(qd, s_prev.astype(BF16), preferred_element_type=F32)
            kd = (kr * kdec_ref[:, c0:c0 + HEAD_DIM]).astype(BF16)
            kv = lax.dot_general(kd, vb, (((0,), (0,)), ((), ())), preferred_element_type=F32)
            state_ref[hh] = chunk_decay[hh] * s_prev + kv
            mu = jnp.mean(o, axis=-1, keepdims=True)
            d = o - mu
            var = jnp.mean(d * d, axis=-1, keepdims=True)
            y = d * lax.rsqrt(var + EPS) * gnw_ref[:, c0:c0 + HEAD_DIM]
            gate = gh * (1.0 / (1.0 + jnp.exp(-gh)))
            mix_ref[r0:r0 + CHUNK, c0:c0 + HEAD_DIM] = (gate * y).astype(BF16)

    while piece < PIECES_IN_RETENTION + PIECES_BEFORE_CONV:
        project_piece(piece)
        piece += 1

    gb = proj[:, 4 * R:4 * R + CONV_WIDTH]
    gc = proj[:, 4 * R + CONV_WIDTH:4 * R + 2 * CONV_WIDTH]
    u = proj[:, 4 * R + 2 * CONV_WIDTH:4 * R + 3 * CONV_WIDTH]
    cu_ref[SUBLANES:SUBLANES + tm, :] = gc * u
    conv = (convw_ref[2:3, :] * cu_ref[SUBLANES:SUBLANES + tm, :]
            + convw_ref[1:2, :] * cu_ref[SUBLANES - 1:SUBLANES - 1 + tm, :]
            + convw_ref[0:1, :] * cu_ref[SUBLANES - 2:SUBLANES - 2 + tm, :])
    mix_ref[:, R:R + CONV_WIDTH] = (gb * conv).astype(BF16)
    cu_ref[0:SUBLANES, :] = cu_ref[tm:tm + SUBLANES, :]

    x1 = x + jnp.dot(mix_ref[...], wo_ref[...], preferred_element_type=F32)
    x1_ref[...] = x1
    h2 = _rms(x1, n2_ref[...])
    h2p_ref[...] = _pack_bf16_pair(h2[:, :PACKED], h2[:, PACKED:])

    hi = h2.astype(BF16)
    lo = (h2 - hi.astype(F32)).astype(BF16)
    wrh = wrh_ref[...]
    logits = (jnp.dot(hi, wrh, preferred_element_type=F32)
              + jnp.dot(lo, wrh, preferred_element_type=F32)
              + jnp.dot(hi, wrl_ref[...], preferred_element_type=F32)
              + rb_ref[...])

    while piece < n_pieces:
        project_piece(piece)
        piece += 1

    lane = lax.broadcasted_iota(I32, (tm, LANES), 1)
    lane_f = lane.astype(F32)
    neg_inf = jnp.float32(-jnp.inf)

    def first_max(v):
        m = jnp.max(v, axis=-1, keepdims=True)
        return m, jnp.min(jnp.where(v == m, lane_f, float(LANES)), axis=-1, keepdims=True)

    gl = jnp.where(lane < N_GROUPS, logits, neg_inf)
    gmax, g_sel = first_max(gl)
    p_group = 1.0 / jnp.sum(jnp.exp(gl - gmax), axis=-1, keepdims=True)

    e_lo = EXPERT_LANE0 + g_sel * EXPERTS_PER_GROUP
    emask = (lane_f >= e_lo) & (lane_f < e_lo + EXPERTS_PER_GROUP)
    el = jnp.where(emask, logits, neg_inf)
    l1, i1 = first_max(el)
    l2, i2 = first_max(jnp.where(lane_f == i1, neg_inf, el))
    r = jnp.exp(l2 - l1)
    gate1 = p_group / (1.0 + r)
    gate2 = gate1 * r

    oh1 = jnp.where(lane_f == i1, 1.0, 0.0)
    oh2 = jnp.where(lane_f == i2, 1.0, 0.0)
    row = lax.broadcasted_iota(I32, (tm, tm), 0)
    col = lax.broadcasted_iota(I32, (tm, tm), 1)
    lower = jnp.where(row > col, 1.0, 0.0).astype(BF16)
    pre1 = jnp.dot(lower, oh1.astype(BF16), preferred_element_type=F32)
    pre2 = jnp.dot(lower, oh2.astype(BF16), preferred_element_type=F32)
    cnt1 = jnp.sum(oh1, axis=0, keepdims=True)
    cnt2 = jnp.sum(oh2, axis=0, keepdims=True)
    run = run_ref[...]
    rank1 = jnp.sum(oh1 * (pre1 + run), axis=-1, keepdims=True)
    rank2 = jnp.sum(oh2 * (pre2 + run + cnt1), axis=-1, keepdims=True)
    run_new = run + cnt1 + cnt2
    run_ref[...] = run_new
    cnt_ref[...] = run_new

    route = jnp.where(lane == 0, i1 - EXPERT_LANE0,
            jnp.where(lane == 1, i2 - EXPERT_LANE0,
            jnp.where(lane == 2, gate1,
            jnp.where(lane == 3, gate2,
            jnp.where(lane == 4, rank1,
            jnp.where(lane == 5, rank2, 0.0))))))
    route_ref[...] = route[:, :SUBLANES]
    route_t_ref[...] = jnp.transpose(route)[:SUBLANES, :]


def _mixer_router(x, n1, win_bf, cosf, sins, dmat, kdec, qdec, gnw, convw, wo_bf, n2, wrh, wrl, rb,
                  chunk_decay):
    B, S, D = x.shape
    tm = MIX_TILE
    n = B * S
    tiles_per_seq = S // tm
    n_tiles = n // tm
    x = x.reshape(n, D)
    const2 = lambda i: (0, 0)
    tile = lambda i: (i, 0)
    in_specs = [
        pl.BlockSpec((tm, D), tile),
        pl.BlockSpec((tm, D), lambda i: (jnp.minimum(i + 1, n_tiles - 1), 0)),
        pl.BlockSpec((1, D), const2),
        pl.BlockSpec((D, IN_COLS), const2),
        pl.BlockSpec((tm, HEAD_DIM), lambda i: (lax.rem(i, tiles_per_seq), 0)),
        pl.BlockSpec((tm, HEAD_DIM), lambda i: (lax.rem(i, tiles_per_seq), 0)),
        pl.BlockSpec((RET_HEADS, CHUNK, CHUNK), lambda i: (0, 0, 0)),
        pl.BlockSpec((CHUNK, RET_WIDTH), const2),
        pl.BlockSpec((CHUNK, RET_WIDTH), const2),
        pl.BlockSpec((1, RET_WIDTH), const2),
        pl.BlockSpec((CONV_K, CONV_WIDTH), const2),
        pl.BlockSpec((D, D), const2),
        pl.BlockSpec((1, D), const2),
        pl.BlockSpec((D, LANES), const2),
        pl.BlockSpec((D, LANES), const2),
        pl.BlockSpec((1, LANES), const2),
    ]
    out_specs = [
        pl.BlockSpec((tm, D), tile),
        pl.BlockSpec((tm, PACKED), tile),
        pl.BlockSpec((tm, SUBLANES), tile),
        pl.BlockSpec((SUBLANES, tm), lambda i: (0, i)),
        pl.BlockSpec((1, LANES), const2),
    ]
    out_shape = [
        jax.ShapeDtypeStruct((n, D), F32),
        jax.ShapeDtypeStruct((n, PACKED), U32),
        jax.ShapeDtypeStruct((n, SUBLANES), F32),
        jax.ShapeDtypeStruct((SUBLANES, n), F32),
        jax.ShapeDtypeStruct((1, LANES), F32),
    ]
    scratch = [
        pltpu.VMEM((RET_HEADS, HEAD_DIM, HEAD_DIM), F32),
        pltpu.VMEM((tm + SUBLANES, CONV_WIDTH), F32),
        pltpu.VMEM((tm, D), BF16),
        pltpu.VMEM((1, LANES), F32),
        pltpu.VMEM((tm, IN_COLS), F32),
        pltpu.VMEM((tm, IN_COLS), F32),
        pltpu.VMEM((tm, D), BF16),
    ]
    return pl.pallas_call(
        functools.partial(_mixer_router_body, chunk_decay=chunk_decay, tiles_per_seq=tiles_per_seq),
        grid=(n_tiles,), in_specs=in_specs, out_specs=out_specs, out_shape=out_shape,
        scratch_shapes=scratch,
        compiler_params=pltpu.CompilerParams(
            dimension_semantics=("arbitrary",), vmem_limit_bytes=VMEM_LIMIT),
        name="mixer_router",
    )(x, x, n1, win_bf, cosf, sins, dmat, kdec, qdec, gnw, convw, wo_bf, n2, wrh, wrl, rb)


def _row_copy(src_ref, src_row, dst_ref, dst_row, sem):
    return pltpu.make_async_copy(src_ref.at[pl.ds(src_row, 1)], dst_ref.at[pl.ds(dst_row, 1)], sem)


def _zero_block_copy(zbuf, dst_ref, blk, sem):
    r0 = pl.multiple_of(blk * MOE_BLOCK, MOE_BLOCK)
    return pltpu.make_async_copy(zbuf, dst_ref.at[pl.ds(r0, MOE_BLOCK)], sem)


def _dispatch_body(dest0_ref, dest1_ref, fill_ref, h2p_ref, xs_ref, zbuf, sem, z_sem):
    tt = h2p_ref.shape[0]

    @pl.when(pl.program_id(0) == 0)
    def _():
        zero = jnp.zeros(zbuf.shape, F32)
        zbuf[...] = _pack_bf16_pair(zero, zero)
        used = fill_ref[N_EXPERTS]
        total = xs_ref.shape[0] // MOE_BLOCK

        def expert_pad(e, carry):
            blk = fill_ref[e]

            @pl.when(blk >= 0)
            def _():
                _zero_block_copy(zbuf, xs_ref, blk, z_sem).start()
            return carry

        def expert_pad_wait(e, carry):
            @pl.when(fill_ref[e] >= 0)
            def _():
                _zero_block_copy(zbuf, xs_ref, 0, z_sem).wait()
            return carry

        def tail(blk, carry):
            _zero_block_copy(zbuf, xs_ref, blk, z_sem).start()
            return carry

        def tail_wait(blk, carry):
            _zero_block_copy(zbuf, xs_ref, 0, z_sem).wait()
            return carry

        lax.fori_loop(0, N_EXPERTS, expert_pad, 0)
        lax.fori_loop(used, total, tail, 0)
        lax.fori_loop(0, N_EXPERTS, expert_pad_wait, 0)
        lax.fori_loop(used, total, tail_wait, 0)

    def issue(t, carry):
        _row_copy(h2p_ref, t, xs_ref, dest0_ref[t], sem).start(priority=0)
        _row_copy(h2p_ref, t, xs_ref, dest1_ref[t], sem).start(priority=1)
        return carry

    lax.fori_loop(0, tt, issue, 0, unroll=8)

    def drain(t, carry):
        _row_copy(h2p_ref, 0, xs_ref, 0, sem).wait()
        _row_copy(h2p_ref, 0, xs_ref, 0, sem).wait()
        return carry

    lax.fori_loop(0, tt, drain, 0, unroll=8)


def _dispatch(dest0, dest1, fill, h2p, xs_rows):
    n = h2p.shape[0]
    tt = ROW_TILE
    return pl.pallas_call(
        _dispatch_body,
        grid=(n // tt,),
        in_specs=[
            pl.BlockSpec((tt,), lambda i: (i,), memory_space=pltpu.SMEM),
            pl.BlockSpec((tt,), lambda i: (i,), memory_space=pltpu.SMEM),
            pl.BlockSpec(memory_space=pltpu.SMEM),
            pl.BlockSpec((tt, PACKED), lambda i: (i, 0)),
        ],
        out_specs=pl.BlockSpec(memory_space=pl.ANY),
        out_shape=jax.ShapeDtypeStruct((xs_rows, PACKED), U32),
        scratch_shapes=[
            pltpu.VMEM((MOE_BLOCK, PACKED), U32),
            pltpu.SemaphoreType.DMA,
            pltpu.SemaphoreType.DMA,
        ],
        compiler_params=pltpu.CompilerParams(dimension_semantics=("arbitrary",)),
        name="dispatch",
    )(dest0, dest1, fill, h2p)


def _experts_body(ce_ref, row0_ref, nblk_ref, meta_ref, xs_ref, wg_ref, wu_ref, wd_ref, ys_ref,
                  xbuf, obuf, zbuf, wgb_ref, wub_ref, wdb_ref, in_sem, out_sem, z_sem):
    i = pl.program_id(0)
    n_chunks = meta_ref[0]
    slot = lax.rem(i, 2)

    def in_copy(chunk, s):
        r0 = pl.multiple_of(row0_ref[chunk], MOE_BLOCK)
        return pltpu.make_async_copy(xs_ref.at[pl.ds(r0, CHUNK_ROWS)], xbuf.at[s], in_sem.at[s])

    def out_copy(chunk, s, piece):
        r0 = pl.multiple_of(row0_ref[chunk] + piece * MOE_BLOCK, MOE_BLOCK)
        return pltpu.make_async_copy(obuf.at[s, pl.ds(piece * MOE_BLOCK, MOE_BLOCK)],
                                     ys_ref.at[pl.ds(r0, MOE_BLOCK)], out_sem.at[s])

    def for_pieces(chunk, fn):
        for piece in range(CHUNK_BLOCKS):
            @pl.when(piece < nblk_ref[chunk])
            def _():
                fn(piece)

    @pl.when(i == 0)
    def _():
        in_copy(0, 0).start()

    @pl.when(i + 1 < n_chunks)
    def _():
        in_copy(i + 1, 1 - slot).start()

    @pl.when(i < n_chunks)
    def _():
        prev = ce_ref[jnp.maximum(i - 1, 0)]

        @pl.when((i == 0) | (ce_ref[i] != prev))
        def _():
            wgb_ref[...] = wg_ref[...].astype(BF16)
            wub_ref[...] = wu_ref[...].astype(BF16)
            wdb_ref[...] = wd_ref[...].astype(BF16)

        in_copy(i, slot).wait()

        @pl.when(i >= 2)
        def _():
            for_pieces(i - 2, lambda piece: out_copy(i - 2, slot, piece).wait())

        for m in range(1, CHUNK_BLOCKS + 1):
            @pl.when(nblk_ref[i] == m)
            def _():
                rows = m * MOE_BLOCK
                lo, hi = _unpack_bf16_pair(xbuf[slot, 0:rows, :])
                xb = jnp.concatenate([lo.astype(BF16), hi.astype(BF16)], axis=-1)
                g = jnp.dot(xb, wgb_ref[...], preferred_element_type=F32)
                u = jnp.dot(xb, wub_ref[...], preferred_element_type=F32)
                hmid = (g * (1.0 / (1.0 + jnp.exp(-g)))) * u
                y = jnp.dot(hmid.astype(BF16), wdb_ref[...], preferred_element_type=F32)
                obuf[slot, 0:rows, :] = _pack_bf16_pair(y[:, :PACKED], y[:, PACKED:])

        for_pieces(i, lambda piece: out_copy(i, slot, piece).start())

    @pl.when(i == n_chunks - 1)
    def _():
        @pl.when(i >= 1)
        def _():
            for_pieces(i - 1, lambda piece: out_copy(i - 1, 1 - slot, piece).wait())
        for_pieces(i, lambda piece: out_copy(i, slot, piece).wait())

        zero = jnp.zeros(zbuf.shape, F32)
        zbuf[...] = _pack_bf16_pair(zero, zero)
        used = meta_ref[1]
        total = ys_ref.shape[0] // MOE_BLOCK

        def z_copy(blk):
            r0 = pl.multiple_of(blk * MOE_BLOCK, MOE_BLOCK)
            return pltpu.make_async_copy(zbuf, ys_ref.at[pl.ds(r0, MOE_BLOCK)], z_sem)

        def z_start(blk, carry):
            z_copy(blk).start()
            return carry

        def z_wait(blk, carry):
            z_copy(blk).wait()
            return carry

        lax.fori_loop(used, total, z_start, 0)
        lax.fori_loop(used, total, z_wait, 0)


def _experts(chunk_e, chunk_row0, chunk_nblk, meta, xs, n_rows_out, w_gate, w_up, w_down):
    w_map = lambda i, ce, r0, nb, mt: (ce[i], 0, 0)
    return pl.pallas_call(
        _experts_body,
        grid_spec=pltpu.PrefetchScalarGridSpec(
            num_scalar_prefetch=4,
            grid=(chunk_e.shape[0],),
            in_specs=[
                pl.BlockSpec(memory_space=pl.ANY),
                pl.BlockSpec((None, D_MODEL, EXPERT_FF), w_map),
                pl.BlockSpec((None, D_MODEL, EXPERT_FF), w_map),
                pl.BlockSpec((None, EXPERT_FF, D_MODEL), w_map),
            ],
            out_specs=pl.BlockSpec(memory_space=pl.ANY),
            scratch_shapes=[
                pltpu.VMEM((2, CHUNK_ROWS, PACKED), U32),
                pltpu.VMEM((2, CHUNK_ROWS, PACKED), U32),
                pltpu.VMEM((MOE_BLOCK, PACKED), U32),
                pltpu.VMEM((D_MODEL, EXPERT_FF), BF16),
                pltpu.VMEM((D_MODEL, EXPERT_FF), BF16),
                pltpu.VMEM((EXPERT_FF, D_MODEL), BF16),
                pltpu.SemaphoreType.DMA((2,)),
                pltpu.SemaphoreType.DMA((2,)),
                pltpu.SemaphoreType.DMA,
            ],
        ),
        out_shape=jax.ShapeDtypeStruct((n_rows_out, PACKED), U32),
        compiler_params=pltpu.CompilerParams(
            dimension_semantics=("arbitrary",), vmem_limit_bytes=VMEM_LIMIT),
        name="experts",
    )(chunk_e, chunk_row0, chunk_nblk, meta, xs, w_gate, w_up, w_down)


def _combine_body(dest0_ref, dest1_ref, x1_ref, route_ref, fw_ref, ys_ref, out_ref, y0_ref, y1_ref, sem):
    tt = x1_ref.shape[0]

    def issue(t, carry):
        _row_copy(ys_ref, dest0_ref[t], y0_ref, t, sem).start(priority=0)
        _row_copy(ys_ref, dest1_ref[t], y1_ref, t, sem).start(priority=1)
        return carry

    lax.fori_loop(0, tt, issue, 0, unroll=8)

    def drain(t, carry):
        _row_copy(ys_ref, 0, y0_ref, 0, sem).wait()
        _row_copy(ys_ref, 0, y1_ref, 0, sem).wait()
        return carry

    lax.fori_loop(0, tt, drain, 0, unroll=8)

    route = route_ref[...]
    g0 = route[:, 2:3]
    g1 = route[:, 3:4]
    a_lo, a_hi = _unpack_bf16_pair(y0_ref[...])
    b_lo, b_hi = _unpack_bf16_pair(y1_ref[...])
    x1 = x1_ref[...]
    z_lo = x1[:, :PACKED] + (a_lo * g0 + b_lo * g1)
    z_hi = x1[:, PACKED:] + (a_hi * g0 + b_hi * g1)
    ms = (jnp.sum(z_lo * z_lo, axis=-1, keepdims=True)
          + jnp.sum(z_hi * z_hi, axis=-1, keepdims=True)) * (1.0 / D_MODEL)
    inv = lax.rsqrt(ms + EPS)
    out_ref[:, :PACKED] = (z_lo * inv) * fw_ref[:, :PACKED]
    out_ref[:, PACKED:] = (z_hi * inv) * fw_ref[:, PACKED:]


def _combine(dest0, dest1, x1, route, fw, ys):
    n, d = x1.shape
    tt = ROW_TILE
    return pl.pallas_call(
        _combine_body,
        grid=(n // tt,),
        in_specs=[
            pl.BlockSpec((tt,), lambda i: (i,), memory_space=pltpu.SMEM),
            pl.BlockSpec((tt,), lambda i: (i,), memory_space=pltpu.SMEM),
            pl.BlockSpec((tt, d), lambda i: (i, 0)),
            pl.BlockSpec((tt, SUBLANES), lambda i: (i, 0)),
            pl.BlockSpec((1, d), lambda i: (0, 0)),
            pl.BlockSpec(memory_space=pl.ANY),
        ],
        out_specs=pl.BlockSpec((tt, d), lambda i: (i, 0)),
        out_shape=jax.ShapeDtypeStruct((n, d), F32),
        scratch_shapes=[
            pltpu.VMEM((tt, PACKED), U32),
            pltpu.VMEM((tt, PACKED), U32),
            pltpu.SemaphoreType.DMA,
        ],
        compiler_params=pltpu.CompilerParams(
            dimension_semantics=("arbitrary",), vmem_limit_bytes=VMEM_LIMIT),
        name="combine",
    )(dest0, dest1, x1, route, fw, ys)


def _retention_tables(seq):
    half = HEAD_DIM // 2
    inv = ROPE_BASE ** (-jnp.arange(half, dtype=F32) / half)
    ang = jnp.arange(seq).astype(F32)[:, None] * inv[None, :]
    cos, sin = jnp.cos(ang), jnp.sin(ang)
    cosf = jnp.concatenate([cos, cos], axis=-1)
    sins = jnp.concatenate([-sin, sin], axis=-1)
    log_g = jnp.log1p(-(2.0 ** (-5.0 - jnp.arange(RET_HEADS, dtype=F32))))
    idx = jnp.arange(CHUNK, dtype=F32)
    rel = idx[:, None] - idx[None, :]
    causal = rel >= 0
    scale = HEAD_DIM ** -0.5
    dmat = jnp.where(causal[None], jnp.exp(log_g[:, None, None] * jnp.where(causal, rel, 0.0)[None]), 0.0)
    k_decay = jnp.exp(log_g[:, None] * (CHUNK - 1 - idx)[None, :])
    q_decay = jnp.exp(log_g[:, None] * (idx + 1)[None, :])
    kdec = jnp.repeat(k_decay.T, HEAD_DIM, axis=1)
    qdec = jnp.repeat(q_decay.T, HEAD_DIM, axis=1) * scale
    return cosf, sins, dmat * scale, kdec, qdec


def _chunk_decay():
    log_g = np.log1p(-(2.0 ** (-5.0 - np.arange(RET_HEADS, dtype=np.float64))))
    return tuple(float(v) for v in np.exp(log_g * CHUNK))


def _layer(x, norm1_w, w_in, ret_gn_w, conv_w, w_o, norm2_w, router_g_w, router_g_b,
           router_e_w, router_e_b, w_gate, w_up, w_down, final_w):
    B, S, D = x.shape
    n = B * S
    cosf, sins, dmat, kdec, qdec = _retention_tables(S)

    wr = jnp.zeros((D, LANES), F32)
    wr = wr.at[:, :N_GROUPS].set(router_g_w)
    wr = wr.at[:, EXPERT_LANE0:EXPERT_LANE0 + N_EXPERTS].set(
        jnp.transpose(router_e_w, (1, 0, 2)).reshape(D, N_EXPERTS))
    wrh = wr.astype(BF16)
    wrl = (wr - wrh.astype(F32)).astype(BF16)
    rb = jnp.zeros((1, LANES), F32)
    rb = rb.at[0, :N_GROUPS].set(router_g_b)
    rb = rb.at[0, EXPERT_LANE0:EXPERT_LANE0 + N_EXPERTS].set(router_e_b.reshape(-1))

    x1, h2p, route, route_t, counts = _mixer_router(
        x, norm1_w[None], w_in.astype(BF16), cosf, sins, dmat, kdec, qdec, ret_gn_w[None], conv_w,
        w_o.astype(BF16), norm2_w[None], wrh, wrl, rb, _chunk_decay())

    expert = route_t[0:2].astype(I32)
    rank = route_t[4:6].astype(I32)
    cnt = counts[0, EXPERT_LANE0:EXPERT_LANE0 + N_EXPERTS].astype(I32)
    padded = ((cnt + MOE_BLOCK - 1) // MOE_BLOCK) * MOE_BLOCK
    pad_ends = jnp.cumsum(padded)
    pad_starts = pad_ends - padded
    eids = jnp.arange(N_EXPERTS, dtype=I32)
    start_of = jnp.sum(jnp.where(expert[:, :, None] == eids, pad_starts, 0), axis=-1)
    dest = start_of + rank
    n_blocks = -(-(n * TOP_K) // MOE_BLOCK) + N_EXPERTS
    used_blocks = pad_ends[-1] // MOE_BLOCK

    n_chunk_slots = -(-(n * TOP_K) // CHUNK_ROWS) + N_EXPERTS
    chunks_of = (padded + CHUNK_ROWS - 1) // CHUNK_ROWS
    chunk_ends = jnp.cumsum(chunks_of)
    n_chunks = chunk_ends[-1]
    ci = jnp.minimum(jnp.arange(n_chunk_slots, dtype=I32), n_chunks - 1)
    chunk_e = jnp.minimum(jnp.sum((chunk_ends[None, :] <= ci[:, None]).astype(I32), axis=1), N_EXPERTS - 1)
    onehot_e = chunk_e[:, None] == eids
    pick = lambda v: jnp.sum(jnp.where(onehot_e, v, 0), axis=-1)
    k = ci - pick(chunk_ends - chunks_of)
    blocks_e = pick(padded) // MOE_BLOCK
    first_nblk = blocks_e - (pick(chunks_of) - 1) * CHUNK_BLOCKS
    chunk_nblk = jnp.where(k == 0, first_nblk, CHUNK_BLOCKS)
    chunk_row0 = pick(pad_starts) + jnp.where(k == 0, 0, first_nblk + (k - 1) * CHUNK_BLOCKS) * MOE_BLOCK
    meta = jnp.stack([n_chunks, used_blocks]).astype(I32)

    xs_rows = n_blocks * MOE_BLOCK + CHUNK_ROWS - MOE_BLOCK
    last_block = jnp.where(padded > 0, pad_ends // MOE_BLOCK - 1, -1)
    fill = jnp.concatenate([last_block, used_blocks[None]]).astype(I32)
    xs = _dispatch(dest[0], dest[1], fill, h2p.reshape(n, PACKED), xs_rows)
    ys = _experts(chunk_e, chunk_row0.astype(I32), chunk_nblk.astype(I32), meta, xs,
                  n_blocks * MOE_BLOCK, w_gate, w_up, w_down)
    out = _combine(dest[0], dest[1], x1.reshape(n, D), route.reshape(n, SUBLANES), final_w[None], ys)
    return out.reshape(B, S, D)


def kernel(x, norm1_w, w_in, ret_gn_w, conv_w, w_o, norm2_w, router_g_w, router_g_b, router_e_w,
           router_e_b, w_gate, w_up, w_down, final_norm_w):
    depth = norm1_w.shape[0]
    assert depth == 1, "the final RMSNorm is fused into the last layer's combine step"
    return _layer(x, norm1_w[0], w_in[0], ret_gn_w[0], conv_w[0], w_o[0], norm2_w[0],
                  router_g_w[0], router_g_b[0], router_e_w[0], router_e_b[0],
                  w_gate[0], w_up[0], w_down[0], final_norm_w)
```

```python
import functools

import jax
import jax.numpy as jnp
import numpy as np
from jax import lax
from jax.experimental import pallas as pl
from jax.experimental.pallas import tpu as pltpu

F32 = jnp.float32
BF16 = jnp.bfloat16
U32 = jnp.uint32
I32 = jnp.int32

D_MODEL = 1024
RET_WIDTH = 512
RET_HEADS = 4
HEAD_DIM = 128
CONV_WIDTH = 512
CONV_K = 3
IN_COLS = 4 * RET_WIDTH + 3 * CONV_WIDTH
CHUNK = 128
ROPE_BASE = 10000.0
N_GROUPS = 8
EXPERTS_PER_GROUP = 8
N_EXPERTS = 64
TOP_K = 2
EXPERT_FF = 512
MOE_BLOCK = 128
EPS = 1e-6

LANES = 128
SUBLANES = 8
PACKED = D_MODEL // 2
ROW_PIECES = PACKED // LANES
EXPERT_LANE0 = N_GROUPS
VMEM_LIMIT = 48 * 1024 * 1024

MIX_TILE = 256
ROW_TILE = 1024
PROJ_PIECE = 512
PIECES_IN_RETENTION = 5
PIECES_AFTER_RETENTION = 0
CHUNK_BLOCKS = 4
CHUNK_ROWS = CHUNK_BLOCKS * MOE_BLOCK


def _pack_bf16_pair(lo, hi):
    return pltpu.pack_elementwise([lo, hi], packed_dtype=BF16)


def _unpack_bf16_pair(packed):
    lo = pltpu.unpack_elementwise(packed, index=0, packed_dtype=BF16, unpacked_dtype=F32)
    hi = pltpu.unpack_elementwise(packed, index=1, packed_dtype=BF16, unpacked_dtype=F32)
    return lo, hi


def _load_rows(src_ref, lead, rows):
    los, his = [], []
    for j in range(ROW_PIECES):
        lo, hi = _unpack_bf16_pair(src_ref[lead + (slice(0, rows), slice(j * LANES, (j + 1) * LANES))])
        los.append(lo)
        his.append(hi)
    return los, his


def _rms(x, w):
    ms = jnp.mean(x * x, axis=-1, keepdims=True)
    return (x * lax.rsqrt(ms + EPS)) * w


def _in_projection(x_ref, n1_ref, win_ref, proj_ref):
    h = _rms(x_ref[...], n1_ref[...])
    proj_ref[...] = jnp.dot(h.astype(BF16), win_ref[...], preferred_element_type=F32)


def _mixer_router_body(x_ref, xn_ref, n1_ref, win_ref, cos_ref, sin_ref, dmat_ref, kdec_ref, qdec_ref,
                       gnw_ref, convw_ref, wo_ref, n2_ref, wrh_ref, wrl_ref, rb_ref,
                       x1_ref, h2p_ref, route_ref, route_t_ref, cnt_ref,
                       state_ref, cu_ref, mix_ref, run_ref, proj_a_ref, proj_b_ref, hn_ref,
                       *, chunk_decay, tiles_per_seq):
    i = pl.program_id(0)
    c = lax.rem(i, tiles_per_seq)

    @pl.when(c == 0)
    def _():
        state_ref[...] = jnp.zeros_like(state_ref)
        cu_ref[0:SUBLANES, :] = jnp.zeros((SUBLANES, CONV_WIDTH), F32)

    @pl.when(i == 0)
    def _():
        run_ref[...] = jnp.zeros_like(run_ref)
        _in_projection(x_ref, n1_ref, win_ref, proj_a_ref)

    def step(proj_ref, proj_next_ref):
        hn_ref[...] = _rms(xn_ref[...], n1_ref[...]).astype(BF16)

        def project_piece(p):
            cols = slice(p * PROJ_PIECE, (p + 1) * PROJ_PIECE)
            proj_next_ref[:, cols] = jnp.dot(hn_ref[...], win_ref[:, cols], preferred_element_type=F32)

        _mix_and_route(proj_ref, x_ref, cos_ref, sin_ref, dmat_ref, kdec_ref, qdec_ref, gnw_ref,
                       convw_ref, wo_ref, n2_ref, wrh_ref, wrl_ref, rb_ref,
                       x1_ref, h2p_ref, route_ref, route_t_ref, cnt_ref,
                       state_ref, cu_ref, mix_ref, run_ref, chunk_decay, project_piece)

    @pl.when(lax.rem(i, 2) == 0)
    def _():
        step(proj_a_ref, proj_b_ref)

    @pl.when(lax.rem(i, 2) == 1)
    def _():
        step(proj_b_ref, proj_a_ref)


def _mix_and_route(proj, x_ref, cos_ref, sin_ref, dmat_ref, kdec_ref, qdec_ref, gnw_ref,
                   convw_ref, wo_ref, n2_ref, wrh_ref, wrl_ref, rb_ref,
                   x1_ref, h2p_ref, route_ref, route_t_ref, cnt_ref,
                   state_ref, cu_ref, mix_ref, run_ref, chunk_decay, project_piece):
    tm = x_ref.shape[0]
    x = x_ref[...]
    R = RET_WIDTH
    cosf = cos_ref[...]
    sins = sin_ref[...]
    n_pieces = IN_COLS // PROJ_PIECE
    piece = 0

    gb = proj[:, 4 * R:4 * R + CONV_WIDTH]
    gc = proj[:, 4 * R + CONV_WIDTH:4 * R + 2 * CONV_WIDTH]
    u = proj[:, 4 * R + 2 * CONV_WIDTH:4 * R + 3 * CONV_WIDTH]
    cu_ref[SUBLANES:SUBLANES + tm, :] = gc * u
    conv = (convw_ref[2:3, :] * cu_ref[SUBLANES:SUBLANES + tm, :]
            + convw_ref[1:2, :] * cu_ref[SUBLANES - 1:SUBLANES - 1 + tm, :]
            + convw_ref[0:1, :] * cu_ref[SUBLANES - 2:SUBLANES - 2 + tm, :])
    cu_ref[0:SUBLANES, :] = cu_ref[tm:tm + SUBLANES, :]
    x1 = x + jnp.dot((gb * conv).astype(BF16), wo_ref[R:R + CONV_WIDTH, :], preferred_element_type=F32)

    heads = range(RET_HEADS)

    def next_piece():
        nonlocal piece
        if piece < PIECES_IN_RETENTION:
            project_piece(piece)
            piece += 1

    for j in range(tm // CHUNK):
        rows = slice(j * CHUNK, (j + 1) * CHUNK)
        cols = [slice(hh * HEAD_DIM, (hh + 1) * HEAD_DIM) for hh in heads]
        cs = cosf[rows, :]
        sn = sins[rows, :]
        qr, kr, vb = [], [], []
        for hh in heads:
            c0 = hh * HEAD_DIM
            qh = proj[rows, c0:c0 + HEAD_DIM]
            kh = proj[rows, R + c0:R + c0 + HEAD_DIM]
            qr.append(qh * cs + pltpu.roll(qh, HEAD_DIM // 2, 1) * sn)
            kr.append(kh * cs + pltpu.roll(kh, HEAD_DIM // 2, 1) * sn)
            vb.append(proj[rows, 2 * R + c0:2 * R + c0 + HEAD_DIM].astype(BF16))
        next_piece()
        scores = [lax.dot_general(qr[hh].astype(BF16), kr[hh].astype(BF16), (((1,), (1,)), ((), ())),
                                  preferred_element_type=F32) for hh in heads]
        s_prev = [state_ref[hh] for hh in heads]
        inter = [jnp.dot((qr[hh] * qdec_ref[:, cols[hh]]).astype(BF16), s_prev[hh].astype(BF16),
                         preferred_element_type=F32) for hh in heads]
        next_piece()
        kv = [lax.dot_general((kr[hh] * kdec_ref[:, cols[hh]]).astype(BF16), vb[hh],
                              (((0,), (0,)), ((), ())), preferred_element_type=F32) for hh in heads]
        for hh in heads:
            state_ref[hh] = chunk_decay[hh] * s_prev[hh] + kv[hh]
        o = [jnp.dot((scores[hh] * dmat_ref[hh]).astype(BF16), vb[hh], preferred_element_type=F32)
             + inter[hh] for hh in heads]
        next_piece()
        mu = [jnp.mean(o[hh], axis=-1, keepdims=True) for hh in heads]
        d = [o[hh] - mu[hh] for hh in heads]
        var = [jnp.mean(d[hh] * d[hh], axis=-1, keepdims=True) for hh in heads]
        for hh in heads:
            gh = proj[rows, 3 * R + hh * HEAD_DIM:3 * R + (hh + 1) * HEAD_DIM]
            y = d[hh] * lax.rsqrt(var[hh] + EPS) * gnw_ref[:, cols[hh]]
            gate = gh * (1.0 / (1.0 + jnp.exp(-gh)))
            mix_ref[rows, cols[hh]] = (gate * y).astype(BF16)

    while piece < PIECES_IN_RETENTION + PIECES_AFTER_RETENTION:
        project_piece(piece)
        piece += 1

    x1 = x1 + jnp.dot(mix_ref[...], wo_ref[0:R, :], preferred_element_type=F32)
    x1_ref[...] = x1
    h2 = _rms(x1, n2_ref[...])
    h2p_ref[...] = _pack_bf16_pair(h2[:, :PACKED], h2[:, PACKED:])

    hi = h2.astype(BF16)
    lo = (h2 - hi.astype(F32)).astype(BF16)
    wrh = wrh_ref[...]
    logits = (jnp.dot(hi, wrh, preferred_element_type=F32)
              + jnp.dot(lo, wrh, preferred_element_type=F32)
              + jnp.dot(hi, wrl_ref[...], preferred_element_type=F32)
              + rb_ref[...])

    while piece < n_pieces:
        project_piece(piece)
        piece += 1

    lane = lax.broadcasted_iota(I32, (tm, LANES), 1)
    lane_f = lane.astype(F32)
    neg_inf = jnp.float32(-jnp.inf)

    def first_max(v):
        m = jnp.max(v, axis=-1, keepdims=True)
        return m, jnp.min(jnp.where(v == m, lane_f, float(LANES)), axis=-1, keepdims=True)

    gl = jnp.where(lane < N_GROUPS, logits, neg_inf)
    gmax, g_sel = first_max(gl)
    p_group = 1.0 / jnp.sum(jnp.exp(gl - gmax), axis=-1, keepdims=True)

    e_lo = EXPERT_LANE0 + g_sel * EXPERTS_PER_GROUP
    emask = (lane_f >= e_lo) & (lane_f < e_lo + EXPERTS_PER_GROUP)
    el = jnp.where(emask, logits, neg_inf)
    l1, i1 = first_max(el)
    l2, i2 = first_max(jnp.where(lane_f == i1, neg_inf, el))
    r = jnp.exp(l2 - l1)
    gate1 = p_group / (1.0 + r)
    gate2 = gate1 * r

    oh1 = jnp.where(lane_f == i1, 1.0, 0.0)
    oh2 = jnp.where(lane_f == i2, 1.0, 0.0)
    row = lax.broadcasted_iota(I32, (tm, tm), 0)
    col = lax.broadcasted_iota(I32, (tm, tm), 1)
    lower = jnp.where(row > col, 1.0, 0.0).astype(BF16)
    pre1 = jnp.dot(lower, oh1.astype(BF16), preferred_element_type=F32)
    pre2 = jnp.dot(lower, oh2.astype(BF16), preferred_element_type=F32)
    cnt1 = jnp.sum(oh1, axis=0, keepdims=True)
    cnt2 = jnp.sum(oh2, axis=0, keepdims=True)
    run = run_ref[...]
    rank1 = jnp.sum(oh1 * (pre1 + run), axis=-1, keepdims=True)
    rank2 = jnp.sum(oh2 * (pre2 + run + cnt1), axis=-1, keepdims=True)
    run_new = run + cnt1 + cnt2
    run_ref[...] = run_new
    cnt_ref[...] = run_new

    route = jnp.where(lane == 0, i1 - EXPERT_LANE0,
            jnp.where(lane == 1, i2 - EXPERT_LANE0,
            jnp.where(lane == 2, gate1,
            jnp.where(lane == 3, gate2,
            jnp.where(lane == 4, rank1,
            jnp.where(lane == 5, rank2, 0.0))))))
    route_ref[...] = route[:, :SUBLANES]
    route_t_ref[...] = jnp.transpose(route)[:SUBLANES, :]


def _mixer_router(x, n1, win_bf, cosf, sins, dmat, kdec, qdec, gnw, convw, wo_bf, n2, wrh, wrl, rb,
                  chunk_decay):
    B, S, D = x.shape
    tm = MIX_TILE
    n = B * S
    tiles_per_seq = S // tm
    n_tiles = n // tm
    x = x.reshape(n, D)
    const2 = lambda i: (0, 0)
    tile = lambda i: (i, 0)
    in_specs = [
        pl.BlockSpec((tm, D), tile),
        pl.BlockSpec((tm, D), lambda i: (jnp.minimum(i + 1, n_tiles - 1), 0)),
        pl.BlockSpec((1, D), const2),
        pl.BlockSpec((D, IN_COLS), const2),
        pl.BlockSpec((tm, HEAD_DIM), lambda i: (lax.rem(i, tiles_per_seq), 0)),
        pl.BlockSpec((tm, HEAD_DIM), lambda i: (lax.rem(i, tiles_per_seq), 0)),
        pl.BlockSpec((RET_HEADS, CHUNK, CHUNK), lambda i: (0, 0, 0)),
        pl.BlockSpec((CHUNK, RET_WIDTH), const2),
        pl.BlockSpec((CHUNK, RET_WIDTH), const2),
        pl.BlockSpec((1, RET_WIDTH), const2),
        pl.BlockSpec((CONV_K, CONV_WIDTH), const2),
        pl.BlockSpec((D, D), const2),
        pl.BlockSpec((1, D), const2),
        pl.BlockSpec((D, LANES), const2),
        pl.BlockSpec((D, LANES), const2),
        pl.BlockSpec((1, LANES), const2),
    ]
    out_specs = [
        pl.BlockSpec((tm, D), tile),
        pl.BlockSpec((tm, PACKED), tile),
        pl.BlockSpec((tm, SUBLANES), tile),
        pl.BlockSpec((SUBLANES, tm), lambda i: (0, i)),
        pl.BlockSpec((1, LANES), const2),
    ]
    out_shape = [
        jax.ShapeDtypeStruct((n, D), F32),
        jax.ShapeDtypeStruct((n, PACKED), U32),
        jax.ShapeDtypeStruct((n, SUBLANES), F32),
        jax.ShapeDtypeStruct((SUBLANES, n), F32),
        jax.ShapeDtypeStruct((1, LANES), F32),
    ]
    scratch = [
        pltpu.VMEM((RET_HEADS, HEAD_DIM, HEAD_DIM), F32),
        pltpu.VMEM((tm + SUBLANES, CONV_WIDTH), F32),
        pltpu.VMEM((tm, RET_WIDTH), BF16),
        pltpu.VMEM((1, LANES), F32),
        pltpu.VMEM((tm, IN_COLS), F32),
        pltpu.VMEM((tm, IN_COLS), F32),
        pltpu.VMEM((tm, D), BF16),
    ]
    return pl.pallas_call(
        functools.partial(_mixer_router_body, chunk_decay=chunk_decay, tiles_per_seq=tiles_per_seq),
        grid=(n_tiles,), in_specs=in_specs, out_specs=out_specs, out_shape=out_shape,
        scratch_shapes=scratch,
        compiler_params=pltpu.CompilerParams(
            dimension_semantics=("arbitrary",), vmem_limit_bytes=VMEM_LIMIT),
        name="mixer_router",
    )(x, x, n1, win_bf, cosf, sins, dmat, kdec, qdec, gnw, convw, wo_bf, n2, wrh, wrl, rb)


def _row_copy(src_ref, src_row, dst_ref, dst_row, sem):
    return pltpu.make_async_copy(src_ref.at[pl.ds(src_row, 1)], dst_ref.at[pl.ds(dst_row, 1)], sem)


def _zero_block_copy(zbuf, dst_ref, blk, sem):
    r0 = pl.multiple_of(blk * MOE_BLOCK, MOE_BLOCK)
    return pltpu.make_async_copy(zbuf, dst_ref.at[pl.ds(r0, MOE_BLOCK)], sem)


def _dispatch_body(dest0_ref, dest1_ref, fill_ref, h2p_ref, xs_ref, zbuf, sem, z_sem):
    tt = h2p_ref.shape[0]

    @pl.when(pl.program_id(0) == 0)
    def _():
        zero = jnp.zeros((MOE_BLOCK, PACKED), F32)
        zbuf[...] = _pack_bf16_pair(zero, zero)
        used = fill_ref[N_EXPERTS]
        total = xs_ref.shape[0] // MOE_BLOCK

        def expert_pad(e, carry):
            blk = fill_ref[e]

            @pl.when(blk >= 0)
            def _():
                _zero_block_copy(zbuf, xs_ref, blk, z_sem).start()
            return carry

        def expert_pad_wait(e, carry):
            @pl.when(fill_ref[e] >= 0)
            def _():
                _zero_block_copy(zbuf, xs_ref, 0, z_sem).wait()
            return carry

        def tail(blk, carry):
            _zero_block_copy(zbuf, xs_ref, blk, z_sem).start()
            return carry

        def tail_wait(blk, carry):
            _zero_block_copy(zbuf, xs_ref, 0, z_sem).wait()
            return carry

        lax.fori_loop(0, N_EXPERTS, expert_pad, 0)
        lax.fori_loop(used, total, tail, 0)
        lax.fori_loop(0, N_EXPERTS, expert_pad_wait, 0)
        lax.fori_loop(used, total, tail_wait, 0)

    def issue(t, carry):
        _row_copy(h2p_ref, t, xs_ref, dest0_ref[t], sem).start(priority=0)
        _row_copy(h2p_ref, t, xs_ref, dest1_ref[t], sem).start(priority=1)
        return carry

    lax.fori_loop(0, tt, issue, 0, unroll=8)

    def drain(t, carry):
        _row_copy(h2p_ref, 0, xs_ref, 0, sem).wait()
        _row_copy(h2p_ref, 0, xs_ref, 0, sem).wait()
        return carry

    lax.fori_loop(0, tt, drain, 0, unroll=8)


def _dispatch(dest0, dest1, fill, h2p, xs_rows):
    n = h2p.shape[0]
    tt = ROW_TILE
    return pl.pallas_call(
        _dispatch_body,
        grid=(n // tt,),
        in_specs=[
            pl.BlockSpec((tt,), lambda i: (i,), memory_space=pltpu.SMEM),
            pl.BlockSpec((tt,), lambda i: (i,), memory_space=pltpu.SMEM),
            pl.BlockSpec(memory_space=pltpu.SMEM),
            pl.BlockSpec((tt, PACKED), lambda i: (i, 0)),
        ],
        out_specs=pl.BlockSpec(memory_space=pl.ANY),
        out_shape=jax.ShapeDtypeStruct((xs_rows, PACKED), U32),
        scratch_shapes=[
            pltpu.VMEM((MOE_BLOCK, PACKED), U32),
            pltpu.SemaphoreType.DMA,
            pltpu.SemaphoreType.DMA,
        ],
        compiler_params=pltpu.CompilerParams(dimension_semantics=("arbitrary",)),
        name="dispatch",
    )(dest0, dest1, fill, h2p)


def _experts_body(ce_ref, row0_ref, nblk_ref, meta_ref, xs_ref, wg_ref, wu_ref, wd_ref, ys_ref,
                  xbuf, obuf, zbuf, wgb_ref, wub_ref, wdb_ref, in_sem, out_sem, z_sem):
    i = pl.program_id(0)
    n_chunks = meta_ref[0]
    slot = lax.rem(i, 2)

    def in_copy(chunk, s):
        r0 = pl.multiple_of(row0_ref[chunk], MOE_BLOCK)
        return pltpu.make_async_copy(xs_ref.at[pl.ds(r0, CHUNK_ROWS)], xbuf.at[s], in_sem.at[s])

    def out_copy(chunk, s, piece):
        r0 = pl.multiple_of(row0_ref[chunk] + piece * MOE_BLOCK, MOE_BLOCK)
        return pltpu.make_async_copy(obuf.at[s, pl.ds(piece * MOE_BLOCK, MOE_BLOCK)],
                                     ys_ref.at[pl.ds(r0, MOE_BLOCK)], out_sem.at[s])

    def for_pieces(chunk, fn):
        for piece in range(CHUNK_BLOCKS):
            @pl.when(piece < nblk_ref[chunk])
            def _():
                fn(piece)

    @pl.when(i == 0)
    def _():
        in_copy(0, 0).start()

    @pl.when(i + 1 < n_chunks)
    def _():
        in_copy(i + 1, 1 - slot).start()

    @pl.when(i < n_chunks)
    def _():
        prev = ce_ref[jnp.maximum(i - 1, 0)]

        @pl.when((i == 0) | (ce_ref[i] != prev))
        def _():
            wgb_ref[...] = wg_ref[...].astype(BF16)
            wub_ref[...] = wu_ref[...].astype(BF16)
            wdb_ref[...] = wd_ref[...].astype(BF16)

        in_copy(i, slot).wait()

        @pl.when(i >= 2)
        def _():
            for_pieces(i - 2, lambda piece: out_copy(i - 2, slot, piece).wait())

        for m in range(1, CHUNK_BLOCKS + 1):
            @pl.when(nblk_ref[i] == m)
            def _():
                rows = m * MOE_BLOCK
                los, his = _load_rows(xbuf, (slot,), rows)
                xb = jnp.concatenate([v.astype(BF16) for v in los + his], axis=-1)
                g = jnp.dot(xb, wgb_ref[...], preferred_element_type=F32)
                u = jnp.dot(xb, wub_ref[...], preferred_element_type=F32)
                hmid = (g * (1.0 / (1.0 + jnp.exp(-g)))) * u
                y = jnp.dot(hmid.astype(BF16), wdb_ref[...], preferred_element_type=F32)
                packed = _pack_bf16_pair(y[:, :PACKED], y[:, PACKED:])
                obuf[slot, 0:rows, :] = packed

        for_pieces(i, lambda piece: out_copy(i, slot, piece).start())

    @pl.when(i == n_chunks - 1)
    def _():
        @pl.when(i >= 1)
        def _():
            for_pieces(i - 1, lambda piece: out_copy(i - 1, 1 - slot, piece).wait())
        for_pieces(i, lambda piece: out_copy(i, slot, piece).wait())

        zero = jnp.zeros((MOE_BLOCK, PACKED), F32)
        zbuf[...] = _pack_bf16_pair(zero, zero)
        used = meta_ref[1]
        total = ys_ref.shape[0] // MOE_BLOCK

        def z_start(blk, carry):
            _zero_block_copy(zbuf, ys_ref, blk, z_sem).start()
            return carry

        def z_wait(blk, carry):
            _zero_block_copy(zbuf, ys_ref, blk, z_sem).wait()
            return carry

        lax.fori_loop(used, total, z_start, 0)
        lax.fori_loop(used, total, z_wait, 0)


def _experts(chunk_e, chunk_row0, chunk_nblk, meta, xs, n_rows_out, w_gate, w_up, w_down):
    w_map = lambda i, ce, r0, nb, mt: (ce[i], 0, 0)
    return pl.pallas_call(
        _experts_body,
        grid_spec=pltpu.PrefetchScalarGridSpec(
            num_scalar_prefetch=4,
            grid=(chunk_e.shape[0],),
            in_specs=[
                pl.BlockSpec(memory_space=pl.ANY),
                pl.BlockSpec((None, D_MODEL, EXPERT_FF), w_map),
                pl.BlockSpec((None, D_MODEL, EXPERT_FF), w_map),
                pl.BlockSpec((None, EXPERT_FF, D_MODEL), w_map),
            ],
            out_specs=pl.BlockSpec(memory_space=pl.ANY),
            scratch_shapes=[
                pltpu.VMEM((2, CHUNK_ROWS, PACKED), U32),
                pltpu.VMEM((2, CHUNK_ROWS, PACKED), U32),
                pltpu.VMEM((MOE_BLOCK, PACKED), U32),
                pltpu.VMEM((D_MODEL, EXPERT_FF), BF16),
                pltpu.VMEM((D_MODEL, EXPERT_FF), BF16),
                pltpu.VMEM((EXPERT_FF, D_MODEL), BF16),
                pltpu.SemaphoreType.DMA((2,)),
                pltpu.SemaphoreType.DMA((2,)),
                pltpu.SemaphoreType.DMA,
            ],
        ),
        out_shape=jax.ShapeDtypeStruct((n_rows_out, PACKED), U32),
        compiler_params=pltpu.CompilerParams(
            dimension_semantics=("arbitrary",), vmem_limit_bytes=VMEM_LIMIT),
        name="experts",
    )(chunk_e, chunk_row0, chunk_nblk, meta, xs, w_gate, w_up, w_down)


def _combine_body(dest0_ref, dest1_ref, x1_ref, route_ref, fw_ref, ys_ref, out_ref, y0_ref, y1_ref, sem):
    tt = x1_ref.shape[0]

    def issue(t, carry):
        _row_copy(ys_ref, dest0_ref[t], y0_ref, t, sem).start(priority=0)
        _row_copy(ys_ref, dest1_ref[t], y1_ref, t, sem).start(priority=1)
        return carry

    lax.fori_loop(0, tt, issue, 0, unroll=8)

    def drain(t, carry):
        _row_copy(ys_ref, 0, y0_ref, 0, sem).wait()
        _row_copy(ys_ref, 0, y1_ref, 0, sem).wait()
        return carry

    lax.fori_loop(0, tt, drain, 0, unroll=8)

    route = route_ref[...]
    g0 = route[:, 2:3]
    g1 = route[:, 3:4]
    a_lo, a_hi = _unpack_bf16_pair(y0_ref[...])
    b_lo, b_hi = _unpack_bf16_pair(y1_ref[...])
    x1 = x1_ref[...]
    z_lo = x1[:, :PACKED] + (a_lo * g0 + b_lo * g1)
    z_hi = x1[:, PACKED:] + (a_hi * g0 + b_hi * g1)
    ms = (jnp.sum(z_lo * z_lo, axis=-1, keepdims=True)
          + jnp.sum(z_hi * z_hi, axis=-1, keepdims=True)) * (1.0 / D_MODEL)
    inv = lax.rsqrt(ms + EPS)
    out_ref[:, :PACKED] = (z_lo * inv) * fw_ref[:, :PACKED]
    out_ref[:, PACKED:] = (z_hi * inv) * fw_ref[:, PACKED:]


def _combine(dest0, dest1, x1, route, fw, ys):
    n, d = x1.shape
    tt = ROW_TILE
    return pl.pallas_call(
        _combine_body,
        grid=(n // tt,),
        in_specs=[
            pl.BlockSpec((tt,), lambda i: (i,), memory_space=pltpu.SMEM),
            pl.BlockSpec((tt,), lambda i: (i,), memory_space=pltpu.SMEM),
            pl.BlockSpec((tt, d), lambda i: (i, 0)),
            pl.BlockSpec((tt, SUBLANES), lambda i: (i, 0)),
            pl.BlockSpec((1, d), lambda i: (0, 0)),
            pl.BlockSpec(memory_space=pl.ANY),
        ],
        out_specs=pl.BlockSpec((tt, d), lambda i: (i, 0)),
        out_shape=jax.ShapeDtypeStruct((n, d), F32),
        scratch_shapes=[
            pltpu.VMEM((tt, PACKED), U32),
            pltpu.VMEM((tt, PACKED), U32),
            pltpu.SemaphoreType.DMA,
        ],
        compiler_params=pltpu.CompilerParams(
            dimension_semantics=("arbitrary",), vmem_limit_bytes=VMEM_LIMIT),
        name="combine",
    )(dest0, dest1, x1, route, fw, ys)


def _retention_tables(seq):
    half = HEAD_DIM // 2
    inv = ROPE_BASE ** (-jnp.arange(half, dtype=F32) / half)
    ang = jnp.arange(seq).astype(F32)[:, None] * inv[None, :]
    cos, sin = jnp.cos(ang), jnp.sin(ang)
    cosf = jnp.concatenate([cos, cos], axis=-1)
    sins = jnp.concatenate([-sin, sin], axis=-1)
    log_g = jnp.log1p(-(2.0 ** (-5.0 - jnp.arange(RET_HEADS, dtype=F32))))
    idx = jnp.arange(CHUNK, dtype=F32)
    rel = idx[:, None] - idx[None, :]
    causal = rel >= 0
    scale = HEAD_DIM ** -0.5
    dmat = jnp.where(causal[None], jnp.exp(log_g[:, None, None] * jnp.where(causal, rel, 0.0)[None]), 0.0)
    k_decay = jnp.exp(log_g[:, None] * (CHUNK - 1 - idx)[None, :])
    q_decay = jnp.exp(log_g[:, None] * (idx + 1)[None, :])
    kdec = jnp.repeat(k_decay.T, HEAD_DIM, axis=1)
    qdec = jnp.repeat(q_decay.T, HEAD_DIM, axis=1) * scale
    return cosf, sins, dmat * scale, kdec, qdec


def _chunk_decay():
    log_g = np.log1p(-(2.0 ** (-5.0 - np.arange(RET_HEADS, dtype=np.float64))))
    return tuple(float(v) for v in np.exp(log_g * CHUNK))


def _layer(x, norm1_w, w_in, ret_gn_w, conv_w, w_o, norm2_w, router_g_w, router_g_b,
           router_e_w, router_e_b, w_gate, w_up, w_down, final_w):
    B, S, D = x.shape
    n = B * S
    cosf, sins, dmat, kdec, qdec = _retention_tables(S)

    wr = jnp.zeros((D, LANES), F32)
    wr = wr.at[:, :N_GROUPS].set(router_g_w)
    wr = wr.at[:, EXPERT_LANE0:EXPERT_LANE0 + N_EXPERTS].set(
        jnp.transpose(router_e_w, (1, 0, 2)).reshape(D, N_EXPERTS))
    wrh = wr.astype(BF16)
    wrl = (wr - wrh.astype(F32)).astype(BF16)
    rb = jnp.zeros((1, LANES), F32)
    rb = rb.at[0, :N_GROUPS].set(router_g_b)
    rb = rb.at[0, EXPERT_LANE0:EXPERT_LANE0 + N_EXPERTS].set(router_e_b.reshape(-1))

    x1, h2p, route, route_t, counts = _mixer_router(
        x, norm1_w[None], w_in.astype(BF16), cosf, sins, dmat, kdec, qdec, ret_gn_w[None], conv_w,
        w_o.astype(BF16), norm2_w[None], wrh, wrl, rb, _chunk_decay())

    expert = route_t[0:2].astype(I32)
    rank = route_t[4:6].astype(I32)
    cnt = counts[0, EXPERT_LANE0:EXPERT_LANE0 + N_EXPERTS].astype(I32)
    padded = ((cnt + MOE_BLOCK - 1) // MOE_BLOCK) * MOE_BLOCK
    pad_ends = jnp.cumsum(padded)
    pad_starts = pad_ends - padded
    eids = jnp.arange(N_EXPERTS, dtype=I32)
    start_of = jnp.sum(jnp.where(expert[:, :, None] == eids, pad_starts, 0), axis=-1)
    dest = start_of + rank
    n_blocks = -(-(n * TOP_K) // MOE_BLOCK) + N_EXPERTS
    used_blocks = pad_ends[-1] // MOE_BLOCK

    n_chunk_slots = -(-(n * TOP_K) // CHUNK_ROWS) + N_EXPERTS
    chunks_of = (padded + CHUNK_ROWS - 1) // CHUNK_ROWS
    chunk_ends = jnp.cumsum(chunks_of)
    n_chunks = chunk_ends[-1]
    ci = jnp.minimum(jnp.arange(n_chunk_slots, dtype=I32), n_chunks - 1)
    chunk_e = jnp.minimum(jnp.sum((chunk_ends[None, :] <= ci[:, None]).astype(I32), axis=1), N_EXPERTS - 1)
    onehot_e = chunk_e[:, None] == eids
    pick = lambda v: jnp.sum(jnp.where(onehot_e, v, 0), axis=-1)
    k = ci - pick(chunk_ends - chunks_of)
    blocks_e = pick(padded) // MOE_BLOCK
    first_nblk = blocks_e - (pick(chunks_of) - 1) * CHUNK_BLOCKS
    chunk_nblk = jnp.where(k == 0, first_nblk, CHUNK_BLOCKS)
    chunk_row0 = pick(pad_starts) + jnp.where(k == 0, 0, first_nblk + (k - 1) * CHUNK_BLOCKS) * MOE_BLOCK
    meta = jnp.stack([n_chunks, used_blocks]).astype(I32)

    xs_rows = n_blocks * MOE_BLOCK + CHUNK_ROWS - MOE_BLOCK
    last_block = jnp.where(padded > 0, pad_ends // MOE_BLOCK - 1, -1)
    fill = jnp.concatenate([last_block, used_blocks[None]]).astype(I32)
    xs = _dispatch(dest[0], dest[1], fill, h2p, xs_rows)
    ys = _experts(chunk_e, chunk_row0.astype(I32), chunk_nblk.astype(I32), meta, xs,
                  n_blocks * MOE_BLOCK, w_gate, w_up, w_down)
    out = _combine(dest[0], dest[1], x1, route, final_w[None], ys)
    return out.reshape(B, S, D)


def kernel(x, norm1_w, w_in, ret_gn_w, conv_w, w_o, norm2_w, router_g_w, router_g_b, router_e_w,
           router_e_b, w_gate, w_up, w_down, final_norm_w):
    depth = norm1_w.shape[0]
    assert depth == 1, "the final RMSNorm is fused into the last layer's combine step"
    return _layer(x, norm1_w[0], w_in[0], ret_gn_w[0], conv_w[0], w_o[0], norm2_w[0],
                  router_g_w[0], router_g_b[0], router_e_w[0], router_e_b[0],
                  w_gate[0], w_up[0], w_down[0], final_norm_w)
```

```python
import functools

import jax
import jax.numpy as jnp
import numpy as np
from jax import lax
from jax.experimental import pallas as pl
from jax.experimental.pallas import tpu as pltpu

F32 = jnp.float32
BF16 = jnp.bfloat16
U32 = jnp.uint32
I32 = jnp.int32

D_MODEL = 1024
RET_WIDTH = 512
RET_HEADS = 4
HEAD_DIM = 128
CONV_WIDTH = 512
CONV_K = 3
IN_COLS = 4 * RET_WIDTH + 3 * CONV_WIDTH
CHUNK = 128
ROPE_BASE = 10000.0
N_GROUPS = 8
EXPERTS_PER_GROUP = 8
N_EXPERTS = 64
TOP_K = 2
EXPERT_FF = 512
MOE_BLOCK = 128
EPS = 1e-6

LANES = 128
SUBLANES = 8
PACKED = D_MODEL // 2
ROW_PIECES = PACKED // LANES


def _quad(rows):
    return rows * ROW_PIECES
EXPERT_LANE0 = N_GROUPS
VMEM_LIMIT = 48 * 1024 * 1024

MIX_TILE = 256
ROW_TILE = 1024
PROJ_PIECE = 512
PIECES_IN_RETENTION = 5
PIECES_AFTER_RETENTION = 0
CHUNK_BLOCKS = 4
CHUNK_ROWS = CHUNK_BLOCKS * MOE_BLOCK


def _pack_bf16_pair(lo, hi):
    return pltpu.pack_elementwise([lo, hi], packed_dtype=BF16)


def _unpack_bf16_pair(packed):
    lo = pltpu.unpack_elementwise(packed, index=0, packed_dtype=BF16, unpacked_dtype=F32)
    hi = pltpu.unpack_elementwise(packed, index=1, packed_dtype=BF16, unpacked_dtype=F32)
    return lo, hi


def _store_rows(dst_ref, lead, rows, packed):
    for j in range(ROW_PIECES):
        dst_ref[lead + (pl.ds(j, rows, stride=ROW_PIECES), slice(None))] = packed[:, j * LANES:(j + 1) * LANES]


def _load_rows(src_ref, lead, rows):
    los, his = [], []
    for j in range(ROW_PIECES):
        lo, hi = _unpack_bf16_pair(src_ref[lead + (pl.ds(j, rows, stride=ROW_PIECES), slice(None))])
        los.append(lo)
        his.append(hi)
    return los, his


def _row_slab(ref, row):
    return ref.at[pl.ds(pl.multiple_of(row * ROW_PIECES, ROW_PIECES), ROW_PIECES)]


def _zero_rows(zbuf):
    zero = jnp.zeros(zbuf.shape, F32)
    zbuf[...] = _pack_bf16_pair(zero, zero)


def _rms(x, w):
    ms = jnp.mean(x * x, axis=-1, keepdims=True)
    return (x * lax.rsqrt(ms + EPS)) * w


def _in_projection(x_ref, n1_ref, win_ref, proj_ref):
    h = _rms(x_ref[...], n1_ref[...])
    proj_ref[...] = jnp.dot(h.astype(BF16), win_ref[...], preferred_element_type=F32)


def _mixer_router_body(x_ref, xn_ref, n1_ref, win_ref, cos_ref, sin_ref, dmat_ref, kdec_ref, qdec_ref,
                       gnw_ref, convw_ref, wo_ref, n2_ref, wrh_ref, wrl_ref, rb_ref,
                       x1_ref, h2p_ref, route_ref, route_t_ref, cnt_ref,
                       state_ref, cu_ref, mix_ref, run_ref, proj_a_ref, proj_b_ref, hn_ref,
                       *, chunk_decay, tiles_per_seq):
    i = pl.program_id(0)
    c = lax.rem(i, tiles_per_seq)

    @pl.when(c == 0)
    def _():
        state_ref[...] = jnp.zeros_like(state_ref)
        cu_ref[0:SUBLANES, :] = jnp.zeros((SUBLANES, CONV_WIDTH), F32)

    @pl.when(i == 0)
    def _():
        run_ref[...] = jnp.zeros_like(run_ref)
        _in_projection(x_ref, n1_ref, win_ref, proj_a_ref)

    def step(proj_ref, proj_next_ref):
        hn_ref[...] = _rms(xn_ref[...], n1_ref[...]).astype(BF16)

        def project_piece(p):
            cols = slice(p * PROJ_PIECE, (p + 1) * PROJ_PIECE)
            proj_next_ref[:, cols] = jnp.dot(hn_ref[...], win_ref[:, cols], preferred_element_type=F32)

        _mix_and_route(proj_ref, x_ref, cos_ref, sin_ref, dmat_ref, kdec_ref, qdec_ref, gnw_ref,
                       convw_ref, wo_ref, n2_ref, wrh_ref, wrl_ref, rb_ref,
                       x1_ref, h2p_ref, route_ref, route_t_ref, cnt_ref,
                       state_ref, cu_ref, mix_ref, run_ref, chunk_decay, project_piece)

    @pl.when(lax.rem(i, 2) == 0)
    def _():
        step(proj_a_ref, proj_b_ref)

    @pl.when(lax.rem(i, 2) == 1)
    def _():
        step(proj_b_ref, proj_a_ref)


def _mix_and_route(proj, x_ref, cos_ref, sin_ref, dmat_ref, kdec_ref, qdec_ref, gnw_ref,
                   convw_ref, wo_ref, n2_ref, wrh_ref, wrl_ref, rb_ref,
                   x1_ref, h2p_ref, route_ref, route_t_ref, cnt_ref,
                   state_ref, cu_ref, mix_ref, run_ref, chunk_decay, project_piece):
    tm = x_ref.shape[0]
    x = x_ref[...]
    R = RET_WIDTH
    cosf = cos_ref[...]
    sins = sin_ref[...]
    n_pieces = IN_COLS // PROJ_PIECE
    piece = 0

    gb = proj[:, 4 * R:4 * R + CONV_WIDTH]
    gc = proj[:, 4 * R + CONV_WIDTH:4 * R + 2 * CONV_WIDTH]
    u = proj[:, 4 * R + 2 * CONV_WIDTH:4 * R + 3 * CONV_WIDTH]
    cu_ref[SUBLANES:SUBLANES + tm, :] = gc * u
    conv = (convw_ref[2:3, :] * cu_ref[SUBLANES:SUBLANES + tm, :]
            + convw_ref[1:2, :] * cu_ref[SUBLANES - 1:SUBLANES - 1 + tm, :]
            + convw_ref[0:1, :] * cu_ref[SUBLANES - 2:SUBLANES - 2 + tm, :])
    cu_ref[0:SUBLANES, :] = cu_ref[tm:tm + SUBLANES, :]
    x1 = x + jnp.dot((gb * conv).astype(BF16), wo_ref[R:R + CONV_WIDTH, :], preferred_element_type=F32)

    heads = range(RET_HEADS)

    def next_piece():
        nonlocal piece
        if piece < PIECES_IN_RETENTION:
            project_piece(piece)
            piece += 1

    for j in range(tm // CHUNK):
        rows = slice(j * CHUNK, (j + 1) * CHUNK)
        cols = [slice(hh * HEAD_DIM, (hh + 1) * HEAD_DIM) for hh in heads]
        cs = cosf[rows, :]
        sn = sins[rows, :]
        qr, kr, vb = [], [], []
        for hh in heads:
            c0 = hh * HEAD_DIM
            qh = proj[rows, c0:c0 + HEAD_DIM]
            kh = proj[rows, R + c0:R + c0 + HEAD_DIM]
            qr.append(qh * cs + pltpu.roll(qh, HEAD_DIM // 2, 1) * sn)
            kr.append(kh * cs + pltpu.roll(kh, HEAD_DIM // 2, 1) * sn)
            vb.append(proj[rows, 2 * R + c0:2 * R + c0 + HEAD_DIM].astype(BF16))
        next_piece()
        scores = [lax.dot_general(qr[hh].astype(BF16), kr[hh].astype(BF16), (((1,), (1,)), ((), ())),
                                  preferred_element_type=F32) for hh in heads]
        s_prev = [state_ref[hh] for hh in heads]
        inter = [jnp.dot((qr[hh] * qdec_ref[:, cols[hh]]).astype(BF16), s_prev[hh].astype(BF16),
                         preferred_element_type=F32) for hh in heads]
        next_piece()
        kv = [lax.dot_general((kr[hh] * kdec_ref[:, cols[hh]]).astype(BF16), vb[hh],
                              (((0,), (0,)), ((), ())), preferred_element_type=F32) for hh in heads]
        for hh in heads:
            state_ref[hh] = chunk_decay[hh] * s_prev[hh] + kv[hh]
        o = [jnp.dot((scores[hh] * dmat_ref[hh]).astype(BF16), vb[hh], preferred_element_type=F32)
             + inter[hh] for hh in heads]
        next_piece()
        mu = [jnp.mean(o[hh], axis=-1, keepdims=True) for hh in heads]
        d = [o[hh] - mu[hh] for hh in heads]
        var = [jnp.mean(d[hh] * d[hh], axis=-1, keepdims=True) for hh in heads]
        for hh in heads:
            gh = proj[rows, 3 * R + hh * HEAD_DIM:3 * R + (hh + 1) * HEAD_DIM]
            y = d[hh] * lax.rsqrt(var[hh] + EPS) * gnw_ref[:, cols[hh]]
            gate = gh * (1.0 / (1.0 + jnp.exp(-gh)))
            mix_ref[rows, cols[hh]] = (gate * y).astype(BF16)

    while piece < PIECES_IN_RETENTION + PIECES_AFTER_RETENTION:
        project_piece(piece)
        piece += 1

    x1 = x1 + jnp.dot(mix_ref[...], wo_ref[0:R, :], preferred_element_type=F32)
    x1_ref[...] = x1
    h2 = _rms(x1, n2_ref[...])
    _store_rows(h2p_ref, (), tm, _pack_bf16_pair(h2[:, :PACKED], h2[:, PACKED:]))

    hi = h2.astype(BF16)
    lo = (h2 - hi.astype(F32)).astype(BF16)
    wrh = wrh_ref[...]
    logits = (jnp.dot(hi, wrh, preferred_element_type=F32)
              + jnp.dot(lo, wrh, preferred_element_type=F32)
              + jnp.dot(hi, wrl_ref[...], preferred_element_type=F32)
              + rb_ref[...])

    while piece < n_pieces:
        project_piece(piece)
        piece += 1

    lane = lax.broadcasted_iota(I32, (tm, LANES), 1)
    lane_f = lane.astype(F32)
    neg_inf = jnp.float32(-jnp.inf)

    def first_max(v):
        m = jnp.max(v, axis=-1, keepdims=True)
        return m, jnp.min(jnp.where(v == m, lane_f, float(LANES)), axis=-1, keepdims=True)

    gl = jnp.where(lane < N_GROUPS, logits, neg_inf)
    gmax, g_sel = first_max(gl)
    p_group = 1.0 / jnp.sum(jnp.exp(gl - gmax), axis=-1, keepdims=True)

    e_lo = EXPERT_LANE0 + g_sel * EXPERTS_PER_GROUP
    emask = (lane_f >= e_lo) & (lane_f < e_lo + EXPERTS_PER_GROUP)
    el = jnp.where(emask, logits, neg_inf)
    l1, i1 = first_max(el)
    l2, i2 = first_max(jnp.where(lane_f == i1, neg_inf, el))
    r = jnp.exp(l2 - l1)
    gate1 = p_group / (1.0 + r)
    gate2 = gate1 * r

    oh1 = jnp.where(lane_f == i1, 1.0, 0.0)
    oh2 = jnp.where(lane_f == i2, 1.0, 0.0)
    row = lax.broadcasted_iota(I32, (tm, tm), 0)
    col = lax.broadcasted_iota(I32, (tm, tm), 1)
    lower = jnp.where(row > col, 1.0, 0.0).astype(BF16)
    pre1 = jnp.dot(lower, oh1.astype(BF16), preferred_element_type=F32)
    pre2 = jnp.dot(lower, oh2.astype(BF16), preferred_element_type=F32)
    cnt1 = jnp.sum(oh1, axis=0, keepdims=True)
    cnt2 = jnp.sum(oh2, axis=0, keepdims=True)
    run = run_ref[...]
    rank1 = jnp.sum(oh1 * (pre1 + run), axis=-1, keepdims=True)
    rank2 = jnp.sum(oh2 * (pre2 + run + cnt1), axis=-1, keepdims=True)
    run_new = run + cnt1 + cnt2
    run_ref[...] = run_new
    cnt_ref[...] = run_new

    route = jnp.where(lane == 0, i1 - EXPERT_LANE0,
            jnp.where(lane == 1, i2 - EXPERT_LANE0,
            jnp.where(lane == 2, gate1,
            jnp.where(lane == 3, gate2,
            jnp.where(lane == 4, rank1,
            jnp.where(lane == 5, rank2, 0.0))))))
    route_ref[...] = route[:, :SUBLANES]
    route_t_ref[...] = jnp.transpose(route)[:SUBLANES, :]


def _mixer_router(x, n1, win_bf, cosf, sins, dmat, kdec, qdec, gnw, convw, wo_bf, n2, wrh, wrl, rb,
                  chunk_decay):
    B, S, D = x.shape
    tm = MIX_TILE
    n = B * S
    tiles_per_seq = S // tm
    n_tiles = n // tm
    x = x.reshape(n, D)
    const2 = lambda i: (0, 0)
    tile = lambda i: (i, 0)
    in_specs = [
        pl.BlockSpec((tm, D), tile),
        pl.BlockSpec((tm, D), lambda i: (jnp.minimum(i + 1, n_tiles - 1), 0)),
        pl.BlockSpec((1, D), const2),
        pl.BlockSpec((D, IN_COLS), const2),
        pl.BlockSpec((tm, HEAD_DIM), lambda i: (lax.rem(i, tiles_per_seq), 0)),
        pl.BlockSpec((tm, HEAD_DIM), lambda i: (lax.rem(i, tiles_per_seq), 0)),
        pl.BlockSpec((RET_HEADS, CHUNK, CHUNK), lambda i: (0, 0, 0)),
        pl.BlockSpec((CHUNK, RET_WIDTH), const2),
        pl.BlockSpec((CHUNK, RET_WIDTH), const2),
        pl.BlockSpec((1, RET_WIDTH), const2),
        pl.BlockSpec((CONV_K, CONV_WIDTH), const2),
        pl.BlockSpec((D, D), const2),
        pl.BlockSpec((1, D), const2),
        pl.BlockSpec((D, LANES), const2),
        pl.BlockSpec((D, LANES), const2),
        pl.BlockSpec((1, LANES), const2),
    ]
    out_specs = [
        pl.BlockSpec((tm, D), tile),
        pl.BlockSpec((_quad(tm), LANES), tile),
        pl.BlockSpec((tm, SUBLANES), tile),
        pl.BlockSpec((SUBLANES, tm), lambda i: (0, i)),
        pl.BlockSpec((1, LANES), const2),
    ]
    out_shape = [
        jax.ShapeDtypeStruct((n, D), F32),
        jax.ShapeDtypeStruct((_quad(n), LANES), U32),
        jax.ShapeDtypeStruct((n, SUBLANES), F32),
        jax.ShapeDtypeStruct((SUBLANES, n), F32),
        jax.ShapeDtypeStruct((1, LANES), F32),
    ]
    scratch = [
        pltpu.VMEM((RET_HEADS, HEAD_DIM, HEAD_DIM), F32),
        pltpu.VMEM((tm + SUBLANES, CONV_WIDTH), F32),
        pltpu.VMEM((tm, RET_WIDTH), BF16),
        pltpu.VMEM((1, LANES), F32),
        pltpu.VMEM((tm, IN_COLS), F32),
        pltpu.VMEM((tm, IN_COLS), F32),
        pltpu.VMEM((tm, D), BF16),
    ]
    return pl.pallas_call(
        functools.partial(_mixer_router_body, chunk_decay=chunk_decay, tiles_per_seq=tiles_per_seq),
        grid=(n_tiles,), in_specs=in_specs, out_specs=out_specs, out_shape=out_shape,
        scratch_shapes=scratch,
        compiler_params=pltpu.CompilerParams(
            dimension_semantics=("arbitrary",), vmem_limit_bytes=VMEM_LIMIT),
        name="mixer_router",
    )(x, x, n1, win_bf, cosf, sins, dmat, kdec, qdec, gnw, convw, wo_bf, n2, wrh, wrl, rb)


def _row_copy(src_ref, src_row, dst_ref, dst_row, sem):
    return pltpu.make_async_copy(_row_slab(src_ref, src_row), _row_slab(dst_ref, dst_row), sem)


def _zero_block_copy(zbuf, dst_ref, blk, sem):
    r0 = pl.multiple_of(blk * _quad(MOE_BLOCK), _quad(MOE_BLOCK))
    return pltpu.make_async_copy(zbuf, dst_ref.at[pl.ds(r0, _quad(MOE_BLOCK))], sem)


def _dispatch_body(dest0_ref, dest1_ref, fill_ref, h2p_ref, xs_ref, zbuf, sem, z_sem):
    tt = dest0_ref.shape[0]

    @pl.when(pl.program_id(0) == 0)
    def _():
        _zero_rows(zbuf)
        used = fill_ref[N_EXPERTS]
        total = xs_ref.shape[0] // _quad(MOE_BLOCK)

        def expert_pad(e, carry):
            blk = fill_ref[e]

            @pl.when(blk >= 0)
            def _():
                _zero_block_copy(zbuf, xs_ref, blk, z_sem).start()
            return carry

        def expert_pad_wait(e, carry):
            @pl.when(fill_ref[e] >= 0)
            def _():
                _zero_block_copy(zbuf, xs_ref, 0, z_sem).wait()
            return carry

        def tail(blk, carry):
            _zero_block_copy(zbuf, xs_ref, blk, z_sem).start()
            return carry

        def tail_wait(blk, carry):
            _zero_block_copy(zbuf, xs_ref, 0, z_sem).wait()
            return carry

        lax.fori_loop(0, N_EXPERTS, expert_pad, 0)
        lax.fori_loop(used, total, tail, 0)
        lax.fori_loop(0, N_EXPERTS, expert_pad_wait, 0)
        lax.fori_loop(used, total, tail_wait, 0)

    def issue(t, carry):
        _row_copy(h2p_ref, t, xs_ref, dest0_ref[t], sem).start(priority=0)
        _row_copy(h2p_ref, t, xs_ref, dest1_ref[t], sem).start(priority=1)
        return carry

    lax.fori_loop(0, tt, issue, 0, unroll=8)

    def drain(t, carry):
        _row_copy(h2p_ref, 0, xs_ref, 0, sem).wait()
        _row_copy(h2p_ref, 0, xs_ref, 0, sem).wait()
        return carry

    lax.fori_loop(0, tt, drain, 0, unroll=8)


def _dispatch(dest0, dest1, fill, h2p, xs_rows):
    n = dest0.shape[0]
    assert h2p.shape == (_quad(n), LANES)
    tt = ROW_TILE
    return pl.pallas_call(
        _dispatch_body,
        grid=(n // tt,),
        in_specs=[
            pl.BlockSpec((tt,), lambda i: (i,), memory_space=pltpu.SMEM),
            pl.BlockSpec((tt,), lambda i: (i,), memory_space=pltpu.SMEM),
            pl.BlockSpec(memory_space=pltpu.SMEM),
            pl.BlockSpec((_quad(tt), LANES), lambda i: (i, 0)),
        ],
        out_specs=pl.BlockSpec(memory_space=pl.ANY),
        out_shape=jax.ShapeDtypeStruct((_quad(xs_rows), LANES), U32),
        scratch_shapes=[
            pltpu.VMEM((_quad(MOE_BLOCK), LANES), U32),
            pltpu.SemaphoreType.DMA,
            pltpu.SemaphoreType.DMA,
        ],
        compiler_params=pltpu.CompilerParams(dimension_semantics=("arbitrary",)),
        name="dispatch",
    )(dest0, dest1, fill, h2p)


def _experts_body(ce_ref, row0_ref, nblk_ref, meta_ref, xs_ref, wg_ref, wu_ref, wd_ref, ys_ref,
                  xbuf, obuf, zbuf, wgb_ref, wub_ref, wdb_ref, in_sem, out_sem, z_sem):
    i = pl.program_id(0)
    n_chunks = meta_ref[0]
    slot = lax.rem(i, 2)

    def in_copy(chunk, s):
        r0 = pl.multiple_of(_quad(row0_ref[chunk]), _quad(MOE_BLOCK))
        return pltpu.make_async_copy(xs_ref.at[pl.ds(r0, _quad(CHUNK_ROWS))], xbuf.at[s], in_sem.at[s])

    def out_copy(chunk, s, piece):
        r0 = pl.multiple_of(_quad(row0_ref[chunk] + piece * MOE_BLOCK), _quad(MOE_BLOCK))
        return pltpu.make_async_copy(obuf.at[s, pl.ds(piece * _quad(MOE_BLOCK), _quad(MOE_BLOCK))],
                                     ys_ref.at[pl.ds(r0, _quad(MOE_BLOCK))], out_sem.at[s])

    def for_pieces(chunk, fn):
        for piece in range(CHUNK_BLOCKS):
            @pl.when(piece < nblk_ref[chunk])
            def _():
                fn(piece)

    @pl.when(i == 0)
    def _():
        in_copy(0, 0).start()

    @pl.when(i + 1 < n_chunks)
    def _():
        in_copy(i + 1, 1 - slot).start()

    @pl.when(i < n_chunks)
    def _():
        prev = ce_ref[jnp.maximum(i - 1, 0)]

        @pl.when((i == 0) | (ce_ref[i] != prev))
        def _():
            wgb_ref[...] = wg_ref[...].astype(BF16)
            wub_ref[...] = wu_ref[...].astype(BF16)
            wdb_ref[...] = wd_ref[...].astype(BF16)

        in_copy(i, slot).wait()

        @pl.when(i >= 2)
        def _():
            for_pieces(i - 2, lambda piece: out_copy(i - 2, slot, piece).wait())

        for m in range(1, CHUNK_BLOCKS + 1):
            @pl.when(nblk_ref[i] == m)
            def _():
                rows = m * MOE_BLOCK
                los, his = _load_rows(xbuf, (slot,), rows)
                xb = jnp.concatenate([v.astype(BF16) for v in los + his], axis=-1)
                g = jnp.dot(xb, wgb_ref[...], preferred_element_type=F32)
                u = jnp.dot(xb, wub_ref[...], preferred_element_type=F32)
                hmid = (g * (1.0 / (1.0 + jnp.exp(-g)))) * u
                y = jnp.dot(hmid.astype(BF16), wdb_ref[...], preferred_element_type=F32)
                packed = _pack_bf16_pair(y[:, :PACKED], y[:, PACKED:])
                _store_rows(obuf, (slot,), rows, packed)

        for_pieces(i, lambda piece: out_copy(i, slot, piece).start())

    @pl.when(i == n_chunks - 1)
    def _():
        @pl.when(i >= 1)
        def _():
            for_pieces(i - 1, lambda piece: out_copy(i - 1, 1 - slot, piece).wait())
        for_pieces(i, lambda piece: out_copy(i, slot, piece).wait())

        _zero_rows(zbuf)
        used = meta_ref[1]
        total = ys_ref.shape[0] // _quad(MOE_BLOCK)

        def z_start(blk, carry):
            _zero_block_copy(zbuf, ys_ref, blk, z_sem).start()
            return carry

        def z_wait(blk, carry):
            _zero_block_copy(zbuf, ys_ref, blk, z_sem).wait()
            return carry

        lax.fori_loop(used, total, z_start, 0)
        lax.fori_loop(used, total, z_wait, 0)


def _experts(chunk_e, chunk_row0, chunk_nblk, meta, xs, n_rows_out, w_gate, w_up, w_down):
    w_map = lambda i, ce, r0, nb, mt: (ce[i], 0, 0)
    return pl.pallas_call(
        _experts_body,
        grid_spec=pltpu.PrefetchScalarGridSpec(
            num_scalar_prefetch=4,
            grid=(chunk_e.shape[0],),
            in_specs=[
                pl.BlockSpec(memory_space=pl.ANY),
                pl.BlockSpec((None, D_MODEL, EXPERT_FF), w_map),
                pl.BlockSpec((None, D_MODEL, EXPERT_FF), w_map),
                pl.BlockSpec((None, EXPERT_FF, D_MODEL), w_map),
            ],
            out_specs=pl.BlockSpec(memory_space=pl.ANY),
            scratch_shapes=[
                pltpu.VMEM((2, _quad(CHUNK_ROWS), LANES), U32),
                pltpu.VMEM((2, _quad(CHUNK_ROWS), LANES), U32),
                pltpu.VMEM((_quad(MOE_BLOCK), LANES), U32),
                pltpu.VMEM((D_MODEL, EXPERT_FF), BF16),
                pltpu.VMEM((D_MODEL, EXPERT_FF), BF16),
                pltpu.VMEM((EXPERT_FF, D_MODEL), BF16),
                pltpu.SemaphoreType.DMA((2,)),
                pltpu.SemaphoreType.DMA((2,)),
                pltpu.SemaphoreType.DMA,
            ],
        ),
        out_shape=jax.ShapeDtypeStruct((_quad(n_rows_out), LANES), U32),
        compiler_params=pltpu.CompilerParams(
            dimension_semantics=("arbitrary",), vmem_limit_bytes=VMEM_LIMIT),
        name="experts",
    )(chunk_e, chunk_row0, chunk_nblk, meta, xs, w_gate, w_up, w_down)


def _combine_body(dest0_ref, dest1_ref, x1_ref, route_ref, fw_ref, ys_ref, out_ref, y0_ref, y1_ref, sem):
    tt = x1_ref.shape[0]

    def issue(t, carry):
        _row_copy(ys_ref, dest0_ref[t], y0_ref, t, sem).start(priority=0)
        _row_copy(ys_ref, dest1_ref[t], y1_ref, t, sem).start(priority=1)
        return carry

    lax.fori_loop(0, tt, issue, 0, unroll=8)

    def drain(t, carry):
        _row_copy(ys_ref, 0, y0_ref, 0, sem).wait()
        _row_copy(ys_ref, 0, y1_ref, 0, sem).wait()
        return carry

    lax.fori_loop(0, tt, drain, 0, unroll=8)

    route = route_ref[...]
    g0 = route[:, 2:3]
    g1 = route[:, 3:4]
    a_lo, a_hi = (jnp.concatenate(v, axis=-1) for v in _load_rows(y0_ref, (), tt))
    b_lo, b_hi = (jnp.concatenate(v, axis=-1) for v in _load_rows(y1_ref, (), tt))
    x1 = x1_ref[...]
    z_lo = x1[:, :PACKED] + (a_lo * g0 + b_lo * g1)
    z_hi = x1[:, PACKED:] + (a_hi * g0 + b_hi * g1)
    ms = (jnp.sum(z_lo * z_lo, axis=-1, keepdims=True)
          + jnp.sum(z_hi * z_hi, axis=-1, keepdims=True)) * (1.0 / D_MODEL)
    inv = lax.rsqrt(ms + EPS)
    out_ref[:, :PACKED] = (z_lo * inv) * fw_ref[:, :PACKED]
    out_ref[:, PACKED:] = (z_hi * inv) * fw_ref[:, PACKED:]


def _combine(dest0, dest1, x1, route, fw, ys):
    n, d = x1.shape
    tt = ROW_TILE
    return pl.pallas_call(
        _combine_body,
        grid=(n // tt,),
        in_specs=[
            pl.BlockSpec((tt,), lambda i: (i,), memory_space=pltpu.SMEM),
            pl.BlockSpec((tt,), lambda i: (i,), memory_space=pltpu.SMEM),
            pl.BlockSpec((tt, d), lambda i: (i, 0)),
            pl.BlockSpec((tt, SUBLANES), lambda i: (i, 0)),
            pl.BlockSpec((1, d), lambda i: (0, 0)),
            pl.BlockSpec(memory_space=pl.ANY),
        ],
        out_specs=pl.BlockSpec((tt, d), lambda i: (i, 0)),
        out_shape=jax.ShapeDtypeStruct((n, d), F32),
        scratch_shapes=[
            pltpu.VMEM((_quad(tt), LANES), U32),
            pltpu.VMEM((_quad(tt), LANES), U32),
            pltpu.SemaphoreType.DMA,
        ],
        compiler_params=pltpu.CompilerParams(
            dimension_semantics=("arbitrary",), vmem_limit_bytes=VMEM_LIMIT),
        name="combine",
    )(dest0, dest1, x1, route, fw, ys)


def _retention_tables(seq):
    half = HEAD_DIM // 2
    inv = ROPE_BASE ** (-jnp.arange(half, dtype=F32) / half)
    ang = jnp.arange(seq).astype(F32)[:, None] * inv[None, :]
    cos, sin = jnp.cos(ang), jnp.sin(ang)
    cosf = jnp.concatenate([cos, cos], axis=-1)
    sins = jnp.concatenate([-sin, sin], axis=-1)
    log_g = jnp.log1p(-(2.0 ** (-5.0 - jnp.arange(RET_HEADS, dtype=F32))))
    idx = jnp.arange(CHUNK, dtype=F32)
    rel = idx[:, None] - idx[None, :]
    causal = rel >= 0
    scale = HEAD_DIM ** -0.5
    dmat = jnp.where(causal[None], jnp.exp(log_g[:, None, None] * jnp.where(causal, rel, 0.0)[None]), 0.0)
    k_decay = jnp.exp(log_g[:, None] * (CHUNK - 1 - idx)[None, :])
    q_decay = jnp.exp(log_g[:, None] * (idx + 1)[None, :])
    kdec = jnp.repeat(k_decay.T, HEAD_DIM, axis=1)
    qdec = jnp.repeat(q_decay.T, HEAD_DIM, axis=1) * scale
    return cosf, sins, dmat * scale, kdec, qdec


def _chunk_decay():
    log_g = np.log1p(-(2.0 ** (-5.0 - np.arange(RET_HEADS, dtype=np.float64))))
    return tuple(float(v) for v in np.exp(log_g * CHUNK))


def _layer(x, norm1_w, w_in, ret_gn_w, conv_w, w_o, norm2_w, router_g_w, router_g_b,
           router_e_w, router_e_b, w_gate, w_up, w_down, final_w):
    B, S, D = x.shape
    n = B * S
    cosf, sins, dmat, kdec, qdec = _retention_tables(S)

    wr = jnp.zeros((D, LANES), F32)
    wr = wr.at[:, :N_GROUPS].set(router_g_w)
    wr = wr.at[:, EXPERT_LANE0:EXPERT_LANE0 + N_EXPERTS].set(
        jnp.transpose(router_e_w, (1, 0, 2)).reshape(D, N_EXPERTS))
    wrh = wr.astype(BF16)
    wrl = (wr - wrh.astype(F32)).astype(BF16)
    rb = jnp.zeros((1, LANES), F32)
    rb = rb.at[0, :N_GROUPS].set(router_g_b)
    rb = rb.at[0, EXPERT_LANE0:EXPERT_LANE0 + N_EXPERTS].set(router_e_b.reshape(-1))

    x1, h2p, route, route_t, counts = _mixer_router(
        x, norm1_w[None], w_in.astype(BF16), cosf, sins, dmat, kdec, qdec, ret_gn_w[None], conv_w,
        w_o.astype(BF16), norm2_w[None], wrh, wrl, rb, _chunk_decay())

    expert = route_t[0:2].astype(I32)
    rank = route_t[4:6].astype(I32)
    cnt = counts[0, EXPERT_LANE0:EXPERT_LANE0 + N_EXPERTS].astype(I32)
    padded = ((cnt + MOE_BLOCK - 1) // MOE_BLOCK) * MOE_BLOCK
    pad_ends = jnp.cumsum(padded)
    pad_starts = pad_ends - padded
    eids = jnp.arange(N_EXPERTS, dtype=I32)
    start_of = jnp.sum(jnp.where(expert[:, :, None] == eids, pad_starts, 0), axis=-1)
    dest = start_of + rank
    n_blocks = -(-(n * TOP_K) // MOE_BLOCK) + N_EXPERTS
    used_blocks = pad_ends[-1] // MOE_BLOCK

    n_chunk_slots = -(-(n * TOP_K) // CHUNK_ROWS) + N_EXPERTS
    chunks_of = (padded + CHUNK_ROWS - 1) // CHUNK_ROWS
    chunk_ends = jnp.cumsum(chunks_of)
    n_chunks = chunk_ends[-1]
    ci = jnp.minimum(jnp.arange(n_chunk_slots, dtype=I32), n_chunks - 1)
    chunk_e = jnp.minimum(jnp.sum((chunk_ends[None, :] <= ci[:, None]).astype(I32), axis=1), N_EXPERTS - 1)
    onehot_e = chunk_e[:, None] == eids
    pick = lambda v: jnp.sum(jnp.where(onehot_e, v, 0), axis=-1)
    k = ci - pick(chunk_ends - chunks_of)
    blocks_e = pick(padded) // MOE_BLOCK
    first_nblk = blocks_e - (pick(chunks_of) - 1) * CHUNK_BLOCKS
    chunk_nblk = jnp.where(k == 0, first_nblk, CHUNK_BLOCKS)
    chunk_row0 = pick(pad_starts) + jnp.where(k == 0, 0, first_nblk + (k - 1) * CHUNK_BLOCKS) * MOE_BLOCK
    meta = jnp.stack([n_chunks, used_blocks]).astype(I32)

    xs_rows = n_blocks * MOE_BLOCK + CHUNK_ROWS - MOE_BLOCK
    last_block = jnp.where(padded > 0, pad_ends // MOE_BLOCK - 1, -1)
    fill = jnp.concatenate([last_block, used_blocks[None]]).astype(I32)
    xs = _dispatch(dest[0], dest[1], fill, h2p, xs_rows)
    ys = _experts(chunk_e, chunk_row0.astype(I32), chunk_nblk.astype(I32), meta, xs,
                  n_blocks * MOE_BLOCK, w_gate, w_up, w_down)
    out = _combine(dest[0], dest[1], x1, route, final_w[None], ys)
    return out.reshape(B, S, D)


def kernel(x, norm1_w, w_in, ret_gn_w, conv_w, w_o, norm2_w, router_g_w, router_g_b, router_e_w,
           router_e_b, w_gate, w_up, w_down, final_norm_w):
    depth = norm1_w.shape[0]
    assert depth == 1, "the final RMSNorm is fused into the last layer's combine step"
    return _layer(x, norm1_w[0], w_in[0], ret_gn_w[0], conv_w[0], w_o[0], norm2_w[0],
                  router_g_w[0], router_g_b[0], router_e_w[0], router_e_b[0],
                  w_gate[0], w_up[0], w_down[0], final_norm_w)
```

```python
import functools

import jax
import jax.numpy as jnp
import numpy as np
from jax import lax
from jax.experimental import pallas as pl
from jax.experimental.pallas import tpu as pltpu

F32 = jnp.float32
BF16 = jnp.bfloat16
U32 = jnp.uint32
I32 = jnp.int32

D_MODEL = 1024
RET_WIDTH = 512
RET_HEADS = 4
HEAD_DIM = 128
CONV_WIDTH = 512
CONV_K = 3
IN_COLS = 4 * RET_WIDTH + 3 * CONV_WIDTH
CHUNK = 128
ROPE_BASE = 10000.0
N_GROUPS = 8
EXPERTS_PER_GROUP = 8
N_EXPERTS = 64
TOP_K = 2
EXPERT_FF = 512
MOE_BLOCK = 128
EPS = 1e-6

LANES = 128
SUBLANES = 8
PACKED = D_MODEL // 2
ROW_PIECES = PACKED // LANES


def _quad(rows):
    return rows * ROW_PIECES
EXPERT_LANE0 = N_GROUPS
VMEM_LIMIT = 48 * 1024 * 1024

MIX_TILE = 256
ROW_TILE = 1024
PROJ_PIECE = 512
PIECES_IN_RETENTION = 5
PIECES_AFTER_RETENTION = 0
CHUNK_BLOCKS = 4
CHUNK_ROWS = CHUNK_BLOCKS * MOE_BLOCK


def _pack_bf16_pair(lo, hi):
    return pltpu.pack_elementwise([lo, hi], packed_dtype=BF16)


def _unpack_bf16_pair(packed):
    lo = pltpu.unpack_elementwise(packed, index=0, packed_dtype=BF16, unpacked_dtype=F32)
    hi = pltpu.unpack_elementwise(packed, index=1, packed_dtype=BF16, unpacked_dtype=F32)
    return lo, hi


def _store_rows(dst_ref, lead, rows, packed):
    for j in range(ROW_PIECES):
        dst_ref[lead + (pl.ds(j, rows, stride=ROW_PIECES), slice(None))] = packed[:, j * LANES:(j + 1) * LANES]


def _load_rows(src_ref, lead, rows):
    los, his = [], []
    for j in range(ROW_PIECES):
        lo, hi = _unpack_bf16_pair(src_ref[lead + (pl.ds(j, rows, stride=ROW_PIECES), slice(None))])
        los.append(lo)
        his.append(hi)
    return los, his


def _row_slab(ref, row):
    return ref.at[pl.ds(pl.multiple_of(row * ROW_PIECES, ROW_PIECES), ROW_PIECES)]


def _zero_rows(zbuf):
    zero = jnp.zeros(zbuf.shape, F32)
    zbuf[...] = _pack_bf16_pair(zero, zero)


def _rms(x, w):
    ms = jnp.mean(x * x, axis=-1, keepdims=True)
    return (x * lax.rsqrt(ms + EPS)) * w


def _in_projection(x_ref, n1_ref, win_ref, proj_ref):
    h = _rms(x_ref[...], n1_ref[...])
    proj_ref[...] = jnp.dot(h.astype(BF16), win_ref[...], preferred_element_type=F32)


def _mixer_router_body(x_ref, xn_ref, n1_ref, win_ref, cos_ref, sin_ref, dmat_ref, kdec_ref, qdec_ref,
                       gnw_ref, convw_ref, wo_ref, n2_ref, wrh_ref, wrl_ref, rb_ref,
                       x1_ref, h2p_ref, route_ref, route_t_ref, cnt_ref,
                       state_ref, cu_ref, mix_ref, run_ref, proj_a_ref, proj_b_ref, hn_ref,
                       *, chunk_decay, tiles_per_seq):
    i = pl.program_id(0)
    c = lax.rem(i, tiles_per_seq)

    @pl.when(c == 0)
    def _():
        state_ref[...] = jnp.zeros_like(state_ref)
        cu_ref[0:SUBLANES, :] = jnp.zeros((SUBLANES, CONV_WIDTH), F32)

    @pl.when(i == 0)
    def _():
        run_ref[...] = jnp.zeros_like(run_ref)
        _in_projection(x_ref, n1_ref, win_ref, proj_a_ref)

    def step(proj_ref, proj_next_ref):
        hn_ref[...] = _rms(xn_ref[...], n1_ref[...]).astype(BF16)

        def project_piece(p):
            cols = slice(p * PROJ_PIECE, (p + 1) * PROJ_PIECE)
            proj_next_ref[:, cols] = jnp.dot(hn_ref[...], win_ref[:, cols], preferred_element_type=F32)

        _mix_and_route(proj_ref, x_ref, cos_ref, sin_ref, dmat_ref, kdec_ref, qdec_ref, gnw_ref,
                       convw_ref, wo_ref, n2_ref, wrh_ref, wrl_ref, rb_ref,
                       x1_ref, h2p_ref, route_ref, route_t_ref, cnt_ref,
                       state_ref, cu_ref, mix_ref, run_ref, chunk_decay, project_piece)

    @pl.when(lax.rem(i, 2) == 0)
    def _():
        step(proj_a_ref, proj_b_ref)

    @pl.when(lax.rem(i, 2) == 1)
    def _():
        step(proj_b_ref, proj_a_ref)


def _mix_and_route(proj, x_ref, cos_ref, sin_ref, dmat_ref, kdec_ref, qdec_ref, gnw_ref,
                   convw_ref, wo_ref, n2_ref, wrh_ref, wrl_ref, rb_ref,
                   x1_ref, h2p_ref, route_ref, route_t_ref, cnt_ref,
                   state_ref, cu_ref, mix_ref, run_ref, chunk_decay, project_piece):
    tm = x_ref.shape[0]
    x = x_ref[...]
    R = RET_WIDTH
    cosf = cos_ref[...]
    sins = sin_ref[...]
    n_pieces = IN_COLS // PROJ_PIECE
    piece = 0

    gb = proj[:, 4 * R:4 * R + CONV_WIDTH]
    gc = proj[:, 4 * R + CONV_WIDTH:4 * R + 2 * CONV_WIDTH]
    u = proj[:, 4 * R + 2 * CONV_WIDTH:4 * R + 3 * CONV_WIDTH]
    cu_ref[SUBLANES:SUBLANES + tm, :] = gc * u
    conv = (convw_ref[2:3, :] * cu_ref[SUBLANES:SUBLANES + tm, :]
            + convw_ref[1:2, :] * cu_ref[SUBLANES - 1:SUBLANES - 1 + tm, :]
            + convw_ref[0:1, :] * cu_ref[SUBLANES - 2:SUBLANES - 2 + tm, :])
    cu_ref[0:SUBLANES, :] = cu_ref[tm:tm + SUBLANES, :]
    x1 = x + jnp.dot((gb * conv).astype(BF16), wo_ref[R:R + CONV_WIDTH, :], preferred_element_type=F32)

    heads = range(RET_HEADS)

    def next_piece():
        nonlocal piece
        if piece < PIECES_IN_RETENTION:
            project_piece(piece)
            piece += 1

    for j in range(tm // CHUNK):
        rows = slice(j * CHUNK, (j + 1) * CHUNK)
        cols = [slice(hh * HEAD_DIM, (hh + 1) * HEAD_DIM) for hh in heads]
        cs = cosf[rows, :]
        sn = sins[rows, :]
        qr, kr, vb = [], [], []
        for hh in heads:
            c0 = hh * HEAD_DIM
            qh = proj[rows, c0:c0 + HEAD_DIM]
            kh = proj[rows, R + c0:R + c0 + HEAD_DIM]
            qr.append(qh * cs + pltpu.roll(qh, HEAD_DIM // 2, 1) * sn)
            kr.append(kh * cs + pltpu.roll(kh, HEAD_DIM // 2, 1) * sn)
            vb.append(proj[rows, 2 * R + c0:2 * R + c0 + HEAD_DIM].astype(BF16))
        next_piece()
        scores = [lax.dot_general(qr[hh].astype(BF16), kr[hh].astype(BF16), (((1,), (1,)), ((), ())),
                                  preferred_element_type=F32) for hh in heads]
        s_prev = [state_ref[hh] for hh in heads]
        inter = [jnp.dot((qr[hh] * qdec_ref[:, cols[hh]]).astype(BF16), s_prev[hh].astype(BF16),
                         preferred_element_type=F32) for hh in heads]
        next_piece()
        kv = [lax.dot_general((kr[hh] * kdec_ref[:, cols[hh]]).astype(BF16), vb[hh],
                              (((0,), (0,)), ((), ())), preferred_element_type=F32) for hh in heads]
        for hh in heads:
            state_ref[hh] = chunk_decay[hh] * s_prev[hh] + kv[hh]
        o = [jnp.dot((scores[hh] * dmat_ref[hh]).astype(BF16), vb[hh], preferred_element_type=F32)
             + inter[hh] for hh in heads]
        next_piece()
        mu = [jnp.mean(o[hh], axis=-1, keepdims=True) for hh in heads]
        d = [o[hh] - mu[hh] for hh in heads]
        var = [jnp.mean(d[hh] * d[hh], axis=-1, keepdims=True) for hh in heads]
        for hh in heads:
            gh = proj[rows, 3 * R + hh * HEAD_DIM:3 * R + (hh + 1) * HEAD_DIM]
            y = d[hh] * lax.rsqrt(var[hh] + EPS) * gnw_ref[:, cols[hh]]
            gate = gh * (1.0 / (1.0 + jnp.exp(-gh)))
            mix_ref[rows, cols[hh]] = (gate * y).astype(BF16)

    while piece < PIECES_IN_RETENTION + PIECES_AFTER_RETENTION:
        project_piece(piece)
        piece += 1

    x1 = x1 + jnp.dot(mix_ref[...], wo_ref[0:R, :], preferred_element_type=F32)
    x1_ref[...] = x1
    h2 = _rms(x1, n2_ref[...])
    _store_rows(h2p_ref, (), tm, _pack_bf16_pair(h2[:, :PACKED], h2[:, PACKED:]))

    hi = h2.astype(BF16)
    lo = (h2 - hi.astype(F32)).astype(BF16)
    wrh = wrh_ref[...]
    logits = (jnp.dot(hi, wrh, preferred_element_type=F32)
              + jnp.dot(lo, wrh, preferred_element_type=F32)
              + jnp.dot(hi, wrl_ref[...], preferred_element_type=F32)
              + rb_ref[...])

    while piece < n_pieces:
        project_piece(piece)
        piece += 1

    lane = lax.broadcasted_iota(I32, (tm, LANES), 1)
    lane_f = lane.astype(F32)
    neg_inf = jnp.float32(-jnp.inf)

    def first_max(v):
        m = jnp.max(v, axis=-1, keepdims=True)
        return m, jnp.min(jnp.where(v == m, lane_f, float(LANES)), axis=-1, keepdims=True)

    gl = jnp.where(lane < N_GROUPS, logits, neg_inf)
    gmax, g_sel = first_max(gl)
    p_group = 1.0 / jnp.sum(jnp.exp(gl - gmax), axis=-1, keepdims=True)

    e_lo = EXPERT_LANE0 + g_sel * EXPERTS_PER_GROUP
    emask = (lane_f >= e_lo) & (lane_f < e_lo + EXPERTS_PER_GROUP)
    el = jnp.where(emask, logits, neg_inf)
    l1, i1 = first_max(el)
    l2, i2 = first_max(jnp.where(lane_f == i1, neg_inf, el))
    r = jnp.exp(l2 - l1)
    gate1 = p_group / (1.0 + r)
    gate2 = gate1 * r

    oh1 = jnp.where(lane_f == i1, 1.0, 0.0)
    oh2 = jnp.where(lane_f == i2, 1.0, 0.0)
    row = lax.broadcasted_iota(I32, (tm, tm), 0)
    col = lax.broadcasted_iota(I32, (tm, tm), 1)
    lower = jnp.where(row > col, 1.0, 0.0).astype(BF16)
    pre1 = jnp.dot(lower, oh1.astype(BF16), preferred_element_type=F32)
    pre2 = jnp.dot(lower, oh2.astype(BF16), preferred_element_type=F32)
    cnt1 = jnp.sum(oh1, axis=0, keepdims=True)
    cnt2 = jnp.sum(oh2, axis=0, keepdims=True)
    run = run_ref[...]
    rank1 = jnp.sum(oh1 * (pre1 + run), axis=-1, keepdims=True)
    rank2 = jnp.sum(oh2 * (pre2 + run + cnt1), axis=-1, keepdims=True)
    run_new = run + cnt1 + cnt2
    run_ref[...] = run_new
    cnt_ref[...] = run_new

    route = jnp.where(lane == 0, i1 - EXPERT_LANE0,
            jnp.where(lane == 1, i2 - EXPERT_LANE0,
            jnp.where(lane == 2, gate1,
            jnp.where(lane == 3, gate2,
            jnp.where(lane == 4, rank1,
            jnp.where(lane == 5, rank2, 0.0))))))
    route_ref[...] = route[:, :SUBLANES]
    route_t_ref[...] = jnp.transpose(route)[:SUBLANES, :]


def _mixer_router(x, n1, win_bf, cosf, sins, dmat, kdec, qdec, gnw, convw, wo_bf, n2, wrh, wrl, rb,
                  chunk_decay):
    B, S, D = x.shape
    tm = MIX_TILE
    n = B * S
    tiles_per_seq = S // tm
    n_tiles = n // tm
    x = x.reshape(n, D)
    const2 = lambda i: (0, 0)
    tile = lambda i: (i, 0)
    in_specs = [
        pl.BlockSpec((tm, D), tile),
        pl.BlockSpec((tm, D), lambda i: (jnp.minimum(i + 1, n_tiles - 1), 0)),
        pl.BlockSpec((1, D), const2),
        pl.BlockSpec((D, IN_COLS), const2),
        pl.BlockSpec((tm, HEAD_DIM), lambda i: (lax.rem(i, tiles_per_seq), 0)),
        pl.BlockSpec((tm, HEAD_DIM), lambda i: (lax.rem(i, tiles_per_seq), 0)),
        pl.BlockSpec((RET_HEADS, CHUNK, CHUNK), lambda i: (0, 0, 0)),
        pl.BlockSpec((CHUNK, RET_WIDTH), const2),
        pl.BlockSpec((CHUNK, RET_WIDTH), const2),
        pl.BlockSpec((1, RET_WIDTH), const2),
        pl.BlockSpec((CONV_K, CONV_WIDTH), const2),
        pl.BlockSpec((D, D), const2),
        pl.BlockSpec((1, D), const2),
        pl.BlockSpec((D, LANES), const2),
        pl.BlockSpec((D, LANES), const2),
        pl.BlockSpec((1, LANES), const2),
    ]
    out_specs = [
        pl.BlockSpec((tm, D), tile),
        pl.BlockSpec((_quad(tm), LANES), tile),
        pl.BlockSpec((tm, SUBLANES), tile),
        pl.BlockSpec((SUBLANES, tm), lambda i: (0, i)),
        pl.BlockSpec((1, LANES), const2),
    ]
    out_shape = [
        jax.ShapeDtypeStruct((n, D), F32),
        jax.ShapeDtypeStruct((_quad(n), LANES), U32),
        jax.ShapeDtypeStruct((n, SUBLANES), F32),
        jax.ShapeDtypeStruct((SUBLANES, n), F32),
        jax.ShapeDtypeStruct((1, LANES), F32),
    ]
    scratch = [
        pltpu.VMEM((RET_HEADS, HEAD_DIM, HEAD_DIM), F32),
        pltpu.VMEM((tm + SUBLANES, CONV_WIDTH), F32),
        pltpu.VMEM((tm, RET_WIDTH), BF16),
        pltpu.VMEM((1, LANES), F32),
        pltpu.VMEM((tm, IN_COLS), F32),
        pltpu.VMEM((tm, IN_COLS), F32),
        pltpu.VMEM((tm, D), BF16),
    ]
    return pl.pallas_call(
        functools.partial(_mixer_router_body, chunk_decay=chunk_decay, tiles_per_seq=tiles_per_seq),
        grid=(n_tiles,), in_specs=in_specs, out_specs=out_specs, out_shape=out_shape,
        scratch_shapes=scratch,
        compiler_params=pltpu.CompilerParams(
            dimension_semantics=("arbitrary",), vmem_limit_bytes=VMEM_LIMIT),
        name="mixer_router",
    )(x, x, n1, win_bf, cosf, sins, dmat, kdec, qdec, gnw, convw, wo_bf, n2, wrh, wrl, rb)


def _row_copy(src_ref, src_row, dst_ref, dst_row, sem):
    return pltpu.make_async_copy(_row_slab(src_ref, src_row), _row_slab(dst_ref, dst_row), sem)


def _zero_block_copy(zbuf, dst_ref, blk, sem):
    r0 = pl.multiple_of(blk * _quad(MOE_BLOCK), _quad(MOE_BLOCK))
    return pltpu.make_async_copy(zbuf, dst_ref.at[pl.ds(r0, _quad(MOE_BLOCK))], sem)


def _dispatch_body(dest0_ref, dest1_ref, fill_ref, h2p_ref, xs_ref, zbuf, sem, z_sem):
    tt = dest0_ref.shape[0]

    @pl.when(pl.program_id(0) == 0)
    def _():
        _zero_rows(zbuf)
        used = fill_ref[N_EXPERTS]
        total = xs_ref.shape[0] // _quad(MOE_BLOCK)

        def expert_pad(e, carry):
            blk = fill_ref[e]

            @pl.when(blk >= 0)
            def _():
                _zero_block_copy(zbuf, xs_ref, blk, z_sem).start()
            return carry

        def expert_pad_wait(e, carry):
            @pl.when(fill_ref[e] >= 0)
            def _():
                _zero_block_copy(zbuf, xs_ref, 0, z_sem).wait()
            return carry

        def tail(blk, carry):
            _zero_block_copy(zbuf, xs_ref, blk, z_sem).start()
            return carry

        def tail_wait(blk, carry):
            _zero_block_copy(zbuf, xs_ref, 0, z_sem).wait()
            return carry

        lax.fori_loop(0, N_EXPERTS, expert_pad, 0)
        lax.fori_loop(used, total, tail, 0)
        lax.fori_loop(0, N_EXPERTS, expert_pad_wait, 0)
        lax.fori_loop(used, total, tail_wait, 0)

    def issue(t, carry):
        _row_copy(h2p_ref, t, xs_ref, dest0_ref[t], sem).start(priority=0)
        _row_copy(h2p_ref, t, xs_ref, dest1_ref[t], sem).start(priority=1)
        return carry

    lax.fori_loop(0, tt, issue, 0, unroll=8)

    def drain(t, carry):
        _row_copy(h2p_ref, 0, xs_ref, 0, sem).wait()
        _row_copy(h2p_ref, 0, xs_ref, 0, sem).wait()
        return carry

    lax.fori_loop(0, tt, drain, 0, unroll=8)


def _dispatch(dest0, dest1, fill, h2p, xs_rows):
    n = dest0.shape[0]
    assert h2p.shape == (_quad(n), LANES)
    tt = ROW_TILE
    return pl.pallas_call(
        _dispatch_body,
        grid=(n // tt,),
        in_specs=[
            pl.BlockSpec((tt,), lambda i: (i,), memory_space=pltpu.SMEM),
            pl.BlockSpec((tt,), lambda i: (i,), memory_space=pltpu.SMEM),
            pl.BlockSpec(memory_space=pltpu.SMEM),
            pl.BlockSpec((_quad(tt), LANES), lambda i: (i, 0)),
        ],
        out_specs=pl.BlockSpec(memory_space=pl.ANY),
        out_shape=jax.ShapeDtypeStruct((_quad(xs_rows), LANES), U32),
        scratch_shapes=[
            pltpu.VMEM((_quad(MOE_BLOCK), LANES), U32),
            pltpu.SemaphoreType.DMA,
            pltpu.SemaphoreType.DMA,
        ],
        compiler_params=pltpu.CompilerParams(dimension_semantics=("arbitrary",)),
        name="dispatch",
    )(dest0, dest1, fill, h2p)


def _experts_body(ce_ref, row0_ref, nblk_ref, meta_ref, xs_ref, wg_ref, wu_ref, wd_ref, ys_ref,
                  xbuf, obuf, zbuf, wgb_ref, wub_ref, wdb_ref, in_sem, out_sem, z_sem):
    i = pl.program_id(0)
    n_chunks = meta_ref[0]
    slot = lax.rem(i, 2)

    def in_copy(chunk, s):
        r0 = pl.multiple_of(_quad(row0_ref[chunk]), _quad(MOE_BLOCK))
        return pltpu.make_async_copy(xs_ref.at[pl.ds(r0, _quad(CHUNK_ROWS))], xbuf.at[s], in_sem.at[s])

    def out_copy(chunk, s, piece):
        r0 = pl.multiple_of(_quad(row0_ref[chunk] + piece * MOE_BLOCK), _quad(MOE_BLOCK))
        return pltpu.make_async_copy(obuf.at[s, pl.ds(piece * _quad(MOE_BLOCK), _quad(MOE_BLOCK))],
                                     ys_ref.at[pl.ds(r0, _quad(MOE_BLOCK))], out_sem.at[s])

    def for_pieces(chunk, fn):
        for piece in range(CHUNK_BLOCKS):
            @pl.when(piece < nblk_ref[chunk])
            def _():
                fn(piece)

    @pl.when(i == 0)
    def _():
        in_copy(0, 0).start()

    @pl.when(i + 1 < n_chunks)
    def _():
        in_copy(i + 1, 1 - slot).start()

    @pl.when(i < n_chunks)
    def _():
        prev = ce_ref[jnp.maximum(i - 1, 0)]

        @pl.when((i == 0) | (ce_ref[i] != prev))
        def _():
            wgb_ref[...] = wg_ref[...].astype(BF16)
            wub_ref[...] = wu_ref[...].astype(BF16)
            wdb_ref[...] = wd_ref[...].astype(BF16)

        in_copy(i, slot).wait()

        @pl.when(i >= 2)
        def _():
            for_pieces(i - 2, lambda piece: out_copy(i - 2, slot, piece).wait())

        for m in range(1, CHUNK_BLOCKS + 1):
            @pl.when(nblk_ref[i] == m)
            def _():
                rows = m * MOE_BLOCK
                los, his = _load_rows(xbuf, (slot,), rows)
                xb = jnp.concatenate([v.astype(BF16) for v in los + his], axis=-1)
                g = jnp.dot(xb, wgb_ref[...], preferred_element_type=F32)
                u = jnp.dot(xb, wub_ref[...], preferred_element_type=F32)
                hmid = (g * (1.0 / (1.0 + jnp.exp(-g)))) * u
                y = jnp.dot(hmid.astype(BF16), wdb_ref[...], preferred_element_type=F32)
                packed = _pack_bf16_pair(y[:, :PACKED], y[:, PACKED:])
                _store_rows(obuf, (slot,), rows, packed)

        for_pieces(i, lambda piece: out_copy(i, slot, piece).start())

    @pl.when(i == n_chunks - 1)
    def _():
        @pl.when(i >= 1)
        def _():
            for_pieces(i - 1, lambda piece: out_copy(i - 1, 1 - slot, piece).wait())
        for_pieces(i, lambda piece: out_copy(i, slot, piece).wait())

        _zero_rows(zbuf)
        used = meta_ref[1]
        total = ys_ref.shape[0] // _quad(MOE_BLOCK)

        def z_start(blk, carry):
            _zero_block_copy(zbuf, ys_ref, blk, z_sem).start()
            return carry

        def z_wait(blk, carry):
            _zero_block_copy(zbuf, ys_ref, blk, z_sem).wait()
            return carry

        lax.fori_loop(used, total, z_start, 0)
        lax.fori_loop(used, total, z_wait, 0)


def _experts(chunk_e, chunk_row0, chunk_nblk, meta, xs, n_rows_out, w_gate, w_up, w_down):
    w_map = lambda i, ce, r0, nb, mt: (ce[i], 0, 0)
    return pl.pallas_call(
        _experts_body,
        grid_spec=pltpu.PrefetchScalarGridSpec(
            num_scalar_prefetch=4,
            grid=(chunk_e.shape[0],),
            in_specs=[
                pl.BlockSpec(memory_space=pl.ANY),
                pl.BlockSpec((None, D_MODEL, EXPERT_FF), w_map),
                pl.BlockSpec((None, D_MODEL, EXPERT_FF), w_map),
                pl.BlockSpec((None, EXPERT_FF, D_MODEL), w_map),
            ],
            out_specs=pl.BlockSpec(memory_space=pl.ANY),
            scratch_shapes=[
                pltpu.VMEM((2, _quad(CHUNK_ROWS), LANES), U32),
                pltpu.VMEM((2, _quad(CHUNK_ROWS), LANES), U32),
                pltpu.VMEM((_quad(MOE_BLOCK), LANES), U32),
                pltpu.VMEM((D_MODEL, EXPERT_FF), BF16),
                pltpu.VMEM((D_MODEL, EXPERT_FF), BF16),
                pltpu.VMEM((EXPERT_FF, D_MODEL), BF16),
                pltpu.SemaphoreType.DMA((2,)),
                pltpu.SemaphoreType.DMA((2,)),
                pltpu.SemaphoreType.DMA,
            ],
        ),
        out_shape=jax.ShapeDtypeStruct((_quad(n_rows_out), LANES), U32),
        compiler_params=pltpu.CompilerParams(
            dimension_semantics=("arbitrary",), vmem_limit_bytes=VMEM_LIMIT),
        name="experts",
    )(chunk_e, chunk_row0, chunk_nblk, meta, xs, w_gate, w_up, w_down)


def _combine_body(dest0_ref, dest1_ref, next0_ref, next1_ref, x1_ref, route_ref, fw_ref, ys_ref, out_ref,
                  y0_ref, y1_ref, sem):
    i = pl.program_id(0)
    tt = x1_ref.shape[0]
    slot = lax.rem(i, 2)

    def gather(d0_ref, d1_ref, s):
        def issue(t, carry):
            _row_copy(ys_ref, d0_ref[t], y0_ref.at[s], t, sem.at[s]).start(priority=0)
            _row_copy(ys_ref, d1_ref[t], y1_ref.at[s], t, sem.at[s]).start(priority=1)
            return carry

        lax.fori_loop(0, tt, issue, 0, unroll=8)

    @pl.when(i == 0)
    def _():
        gather(dest0_ref, dest1_ref, 0)

    @pl.when(i + 1 < pl.num_programs(0))
    def _():
        gather(next0_ref, next1_ref, 1 - slot)

    def drain(t, carry):
        _row_copy(ys_ref, 0, y0_ref.at[slot], 0, sem.at[slot]).wait()
        _row_copy(ys_ref, 0, y1_ref.at[slot], 0, sem.at[slot]).wait()
        return carry

    lax.fori_loop(0, tt, drain, 0, unroll=8)

    route = route_ref[...]
    g0 = route[:, 2:3]
    g1 = route[:, 3:4]
    a_lo, a_hi = (jnp.concatenate(v, axis=-1) for v in _load_rows(y0_ref, (slot,), tt))
    b_lo, b_hi = (jnp.concatenate(v, axis=-1) for v in _load_rows(y1_ref, (slot,), tt))
    x1 = x1_ref[...]
    z_lo = x1[:, :PACKED] + (a_lo * g0 + b_lo * g1)
    z_hi = x1[:, PACKED:] + (a_hi * g0 + b_hi * g1)
    ms = (jnp.sum(z_lo * z_lo, axis=-1, keepdims=True)
          + jnp.sum(z_hi * z_hi, axis=-1, keepdims=True)) * (1.0 / D_MODEL)
    inv = lax.rsqrt(ms + EPS)
    out_ref[:, :PACKED] = (z_lo * inv) * fw_ref[:, :PACKED]
    out_ref[:, PACKED:] = (z_hi * inv) * fw_ref[:, PACKED:]


def _combine(dest0, dest1, x1, route, fw, ys):
    n, d = x1.shape
    tt = ROW_TILE
    steps = n // tt
    this_tile = lambda i: (i,)
    next_tile = lambda i: (jnp.minimum(i + 1, steps - 1),)
    return pl.pallas_call(
        _combine_body,
        grid=(steps,),
        in_specs=[
            pl.BlockSpec((tt,), this_tile, memory_space=pltpu.SMEM),
            pl.BlockSpec((tt,), this_tile, memory_space=pltpu.SMEM),
            pl.BlockSpec((tt,), next_tile, memory_space=pltpu.SMEM),
            pl.BlockSpec((tt,), next_tile, memory_space=pltpu.SMEM),
            pl.BlockSpec((tt, d), lambda i: (i, 0)),
            pl.BlockSpec((tt, SUBLANES), lambda i: (i, 0)),
            pl.BlockSpec((1, d), lambda i: (0, 0)),
            pl.BlockSpec(memory_space=pl.ANY),
        ],
        out_specs=pl.BlockSpec((tt, d), lambda i: (i, 0)),
        out_shape=jax.ShapeDtypeStruct((n, d), F32),
        scratch_shapes=[
            pltpu.VMEM((2, _quad(tt), LANES), U32),
            pltpu.VMEM((2, _quad(tt), LANES), U32),
            pltpu.SemaphoreType.DMA((2,)),
        ],
        compiler_params=pltpu.CompilerParams(
            dimension_semantics=("arbitrary",), vmem_limit_bytes=VMEM_LIMIT),
        name="combine",
    )(dest0, dest1, dest0, dest1, x1, route, fw, ys)


def _retention_tables(seq):
    half = HEAD_DIM // 2
    inv = ROPE_BASE ** (-jnp.arange(half, dtype=F32) / half)
    ang = jnp.arange(seq).astype(F32)[:, None] * inv[None, :]
    cos, sin = jnp.cos(ang), jnp.sin(ang)
    cosf = jnp.concatenate([cos, cos], axis=-1)
    sins = jnp.concatenate([-sin, sin], axis=-1)
    log_g = jnp.log1p(-(2.0 ** (-5.0 - jnp.arange(RET_HEADS, dtype=F32))))
    idx = jnp.arange(CHUNK, dtype=F32)
    rel = idx[:, None] - idx[None, :]
    causal = rel >= 0
    scale = HEAD_DIM ** -0.5
    dmat = jnp.where(causal[None], jnp.exp(log_g[:, None, None] * jnp.where(causal, rel, 0.0)[None]), 0.0)
    k_decay = jnp.exp(log_g[:, None] * (CHUNK - 1 - idx)[None, :])
    q_decay = jnp.exp(log_g[:, None] * (idx + 1)[None, :])
    kdec = jnp.repeat(k_decay.T, HEAD_DIM, axis=1)
    qdec = jnp.repeat(q_decay.T, HEAD_DIM, axis=1) * scale
    return cosf, sins, dmat * scale, kdec, qdec


def _chunk_decay():
    log_g = np.log1p(-(2.0 ** (-5.0 - np.arange(RET_HEADS, dtype=np.float64))))
    return tuple(float(v) for v in np.exp(log_g * CHUNK))


def _layer(x, norm1_w, w_in, ret_gn_w, conv_w, w_o, norm2_w, router_g_w, router_g_b,
           router_e_w, router_e_b, w_gate, w_up, w_down, final_w):
    B, S, D = x.shape
    n = B * S
    cosf, sins, dmat, kdec, qdec = _retention_tables(S)

    wr = jnp.zeros((D, LANES), F32)
    wr = wr.at[:, :N_GROUPS].set(router_g_w)
    wr = wr.at[:, EXPERT_LANE0:EXPERT_LANE0 + N_EXPERTS].set(
        jnp.transpose(router_e_w, (1, 0, 2)).reshape(D, N_EXPERTS))
    wrh = wr.astype(BF16)
    wrl = (wr - wrh.astype(F32)).astype(BF16)
    rb = jnp.zeros((1, LANES), F32)
    rb = rb.at[0, :N_GROUPS].set(router_g_b)
    rb = rb.at[0, EXPERT_LANE0:EXPERT_LANE0 + N_EXPERTS].set(router_e_b.reshape(-1))

    x1, h2p, route, route_t, counts = _mixer_router(
        x, norm1_w[None], w_in.astype(BF16), cosf, sins, dmat, kdec, qdec, ret_gn_w[None], conv_w,
        w_o.astype(BF16), norm2_w[None], wrh, wrl, rb, _chunk_decay())

    expert = route_t[0:2].astype(I32)
    rank = route_t[4:6].astype(I32)
    cnt = counts[0, EXPERT_LANE0:EXPERT_LANE0 + N_EXPERTS].astype(I32)
    padded = ((cnt + MOE_BLOCK - 1) // MOE_BLOCK) * MOE_BLOCK
    pad_ends = jnp.cumsum(padded)
    pad_starts = pad_ends - padded
    eids = jnp.arange(N_EXPERTS, dtype=I32)
    start_of = jnp.sum(jnp.where(expert[:, :, None] == eids, pad_starts, 0), axis=-1)
    dest = start_of + rank
    n_blocks = -(-(n * TOP_K) // MOE_BLOCK) + N_EXPERTS
    used_blocks = pad_ends[-1] // MOE_BLOCK

    n_chunk_slots = -(-(n * TOP_K) // CHUNK_ROWS) + N_EXPERTS
    chunks_of = (padded + CHUNK_ROWS - 1) // CHUNK_ROWS
    chunk_ends = jnp.cumsum(chunks_of)
    n_chunks = chunk_ends[-1]
    ci = jnp.minimum(jnp.arange(n_chunk_slots, dtype=I32), n_chunks - 1)
    chunk_e = jnp.minimum(jnp.sum((chunk_ends[None, :] <= ci[:, None]).astype(I32), axis=1), N_EXPERTS - 1)
    onehot_e = chunk_e[:, None] == eids
    pick = lambda v: jnp.sum(jnp.where(onehot_e, v, 0), axis=-1)
    k = ci - pick(chunk_ends - chunks_of)
    blocks_e = pick(padded) // MOE_BLOCK
    first_nblk = blocks_e - (pick(chunks_of) - 1) * CHUNK_BLOCKS
    chunk_nblk = jnp.where(k == 0, first_nblk, CHUNK_BLOCKS)
    chunk_row0 = pick(pad_starts) + jnp.where(k == 0, 0, first_nblk + (k - 1) * CHUNK_BLOCKS) * MOE_BLOCK
    meta = jnp.stack([n_chunks, used_blocks]).astype(I32)

    xs_rows = n_blocks * MOE_BLOCK + CHUNK_ROWS - MOE_BLOCK
    last_block = jnp.where(padded > 0, pad_ends // MOE_BLOCK - 1, -1)
    fill = jnp.concatenate([last_block, used_blocks[None]]).astype(I32)
    xs = _dispatch(dest[0], dest[1], fill, h2p, xs_rows)
    ys = _experts(chunk_e, chunk_row0.astype(I32), chunk_nblk.astype(I32), meta, xs,
                  n_blocks * MOE_BLOCK, w_gate, w_up, w_down)
    out = _combine(dest[0], dest[1], x1, route, final_w[None], ys)
    return out.reshape(B, S, D)


def kernel(x, norm1_w, w_in, ret_gn_w, conv_w, w_o, norm2_w, router_g_w, router_g_b, router_e_w,
           router_e_b, w_gate, w_up, w_down, final_norm_w):
    depth = norm1_w.shape[0]
    assert depth == 1, "the final RMSNorm is fused into the last layer's combine step"
    return _layer(x, norm1_w[0], w_in[0], ret_gn_w[0], conv_w[0], w_o[0], norm2_w[0],
                  router_g_w[0], router_g_b[0], router_e_w[0], router_e_b[0],
                  w_gate[0], w_up[0], w_down[0], final_norm_w)
```

```python
import functools

import jax
import jax.numpy as jnp
import numpy as np
from jax import lax
from jax.experimental import pallas as pl
from jax.experimental.pallas import tpu as pltpu

F32 = jnp.float32
BF16 = jnp.bfloat16
U32 = jnp.uint32
I32 = jnp.int32

D_MODEL = 1024
RET_WIDTH = 512
RET_HEADS = 4
HEAD_DIM = 128
CONV_WIDTH = 512
CONV_K = 3
IN_COLS = 4 * RET_WIDTH + 3 * CONV_WIDTH
CHUNK = 128
ROPE_BASE = 10000.0
N_GROUPS = 8
EXPERTS_PER_GROUP = 8
N_EXPERTS = 64
TOP_K = 2
EXPERT_FF = 512
MOE_BLOCK = 128
EPS = 1e-6

LANES = 128
SUBLANES = 8
PACKED = D_MODEL // 2
ROW_PIECES = PACKED // LANES


def _quad(rows):
    return rows * ROW_PIECES
EXPERT_LANE0 = N_GROUPS
VMEM_LIMIT = 48 * 1024 * 1024

MIX_TILE = 256
ROW_TILE = 1024
PROJ_PIECE = 512
PIECES_IN_RETENTION = 5
PIECES_AFTER_RETENTION = 0
CHUNK_BLOCKS = 4
CHUNK_ROWS = CHUNK_BLOCKS * MOE_BLOCK
WEIGHT_SLOTS = 3


def _pack_bf16_pair(lo, hi):
    return pltpu.pack_elementwise([lo, hi], packed_dtype=BF16)


def _unpack_bf16_pair(packed):
    lo = pltpu.unpack_elementwise(packed, index=0, packed_dtype=BF16, unpacked_dtype=F32)
    hi = pltpu.unpack_elementwise(packed, index=1, packed_dtype=BF16, unpacked_dtype=F32)
    return lo, hi


def _store_rows(dst_ref, lead, rows, packed):
    for j in range(ROW_PIECES):
        dst_ref[lead + (pl.ds(j, rows, stride=ROW_PIECES), slice(None))] = packed[:, j * LANES:(j + 1) * LANES]


def _load_rows(src_ref, lead, rows):
    los, his = [], []
    for j in range(ROW_PIECES):
        lo, hi = _unpack_bf16_pair(src_ref[lead + (pl.ds(j, rows, stride=ROW_PIECES), slice(None))])
        los.append(lo)
        his.append(hi)
    return los, his


def _row_slab(ref, row):
    return ref.at[pl.ds(pl.multiple_of(row * ROW_PIECES, ROW_PIECES), ROW_PIECES)]


def _zero_rows(zbuf):
    zero = jnp.zeros(zbuf.shape, F32)
    zbuf[...] = _pack_bf16_pair(zero, zero)


def _rms(x, w):
    ms = jnp.mean(x * x, axis=-1, keepdims=True)
    return (x * lax.rsqrt(ms + EPS)) * w


def _in_projection(x_ref, n1_ref, win_ref, proj_ref):
    h = _rms(x_ref[...], n1_ref[...])
    proj_ref[...] = jnp.dot(h.astype(BF16), win_ref[...], preferred_element_type=F32)


def _mixer_router_body(x_ref, xn_ref, n1_ref, win_ref, cos_ref, sin_ref, dmat_ref, kdec_ref, qdec_ref,
                       gnw_ref, convw_ref, wo_ref, n2_ref, wrh_ref, wrl_ref, rb_ref,
                       x1_ref, h2p_ref, route_ref, route_t_ref, cnt_ref,
                       state_ref, cu_ref, mix_ref, run_ref, proj_a_ref, proj_b_ref, hn_ref,
                       *, chunk_decay, tiles_per_seq):
    i = pl.program_id(0)
    c = lax.rem(i, tiles_per_seq)

    @pl.when(c == 0)
    def _():
        state_ref[...] = jnp.zeros_like(state_ref)
        cu_ref[0:SUBLANES, :] = jnp.zeros((SUBLANES, CONV_WIDTH), F32)

    @pl.when(i == 0)
    def _():
        run_ref[...] = jnp.zeros_like(run_ref)
        _in_projection(x_ref, n1_ref, win_ref, proj_a_ref)

    def step(proj_ref, proj_next_ref):
        hn_ref[...] = _rms(xn_ref[...], n1_ref[...]).astype(BF16)

        def project_piece(p):
            cols = slice(p * PROJ_PIECE, (p + 1) * PROJ_PIECE)
            proj_next_ref[:, cols] = jnp.dot(hn_ref[...], win_ref[:, cols], preferred_element_type=F32)

        _mix_and_route(proj_ref, x_ref, cos_ref, sin_ref, dmat_ref, kdec_ref, qdec_ref, gnw_ref,
                       convw_ref, wo_ref, n2_ref, wrh_ref, wrl_ref, rb_ref,
                       x1_ref, h2p_ref, route_ref, route_t_ref, cnt_ref,
                       state_ref, cu_ref, mix_ref, run_ref, chunk_decay, project_piece)

    @pl.when(lax.rem(i, 2) == 0)
    def _():
        step(proj_a_ref, proj_b_ref)

    @pl.when(lax.rem(i, 2) == 1)
    def _():
        step(proj_b_ref, proj_a_ref)


def _mix_and_route(proj, x_ref, cos_ref, sin_ref, dmat_ref, kdec_ref, qdec_ref, gnw_ref,
                   convw_ref, wo_ref, n2_ref, wrh_ref, wrl_ref, rb_ref,
                   x1_ref, h2p_ref, route_ref, route_t_ref, cnt_ref,
                   state_ref, cu_ref, mix_ref, run_ref, chunk_decay, project_piece):
    tm = x_ref.shape[0]
    x = x_ref[...]
    R = RET_WIDTH
    cosf = cos_ref[...]
    sins = sin_ref[...]
    n_pieces = IN_COLS // PROJ_PIECE
    piece = 0

    gb = proj[:, 4 * R:4 * R + CONV_WIDTH]
    gc = proj[:, 4 * R + CONV_WIDTH:4 * R + 2 * CONV_WIDTH]
    u = proj[:, 4 * R + 2 * CONV_WIDTH:4 * R + 3 * CONV_WIDTH]
    cu_ref[SUBLANES:SUBLANES + tm, :] = gc * u
    conv = (convw_ref[2:3, :] * cu_ref[SUBLANES:SUBLANES + tm, :]
            + convw_ref[1:2, :] * cu_ref[SUBLANES - 1:SUBLANES - 1 + tm, :]
            + convw_ref[0:1, :] * cu_ref[SUBLANES - 2:SUBLANES - 2 + tm, :])
    cu_ref[0:SUBLANES, :] = cu_ref[tm:tm + SUBLANES, :]
    x1 = x + jnp.dot((gb * conv).astype(BF16), wo_ref[R:R + CONV_WIDTH, :], preferred_element_type=F32)

    heads = range(RET_HEADS)

    def next_piece():
        nonlocal piece
        if piece < PIECES_IN_RETENTION:
            project_piece(piece)
            piece += 1

    for j in range(tm // CHUNK):
        rows = slice(j * CHUNK, (j + 1) * CHUNK)
        cols = [slice(hh * HEAD_DIM, (hh + 1) * HEAD_DIM) for hh in heads]
        cs = cosf[rows, :]
        sn = sins[rows, :]
        qr, kr, vb = [], [], []
        for hh in heads:
            c0 = hh * HEAD_DIM
            qh = proj[rows, c0:c0 + HEAD_DIM]
            kh = proj[rows, R + c0:R + c0 + HEAD_DIM]
            qr.append(qh * cs + pltpu.roll(qh, HEAD_DIM // 2, 1) * sn)
            kr.append(kh * cs + pltpu.roll(kh, HEAD_DIM // 2, 1) * sn)
            vb.append(proj[rows, 2 * R + c0:2 * R + c0 + HEAD_DIM].astype(BF16))
        next_piece()
        scores = [lax.dot_general(qr[hh].astype(BF16), kr[hh].astype(BF16), (((1,), (1,)), ((), ())),
                                  preferred_element_type=F32) for hh in heads]
        s_prev = [state_ref[hh] for hh in heads]
        inter = [jnp.dot((qr[hh] * qdec_ref[:, cols[hh]]).astype(BF16), s_prev[hh].astype(BF16),
                         preferred_element_type=F32) for hh in heads]
        next_piece()
        kv = [lax.dot_general((kr[hh] * kdec_ref[:, cols[hh]]).astype(BF16), vb[hh],
                              (((0,), (0,)), ((), ())), preferred_element_type=F32) for hh in heads]
        for hh in heads:
            state_ref[hh] = chunk_decay[hh] * s_prev[hh] + kv[hh]
        o = [jnp.dot((scores[hh] * dmat_ref[hh]).astype(BF16), vb[hh], preferred_element_type=F32)
             + inter[hh] for hh in heads]
        next_piece()
        mu = [jnp.mean(o[hh], axis=-1, keepdims=True) for hh in heads]
        d = [o[hh] - mu[hh] for hh in heads]
        var = [jnp.mean(d[hh] * d[hh], axis=-1, keepdims=True) for hh in heads]
        for hh in heads:
            gh = proj[rows, 3 * R + hh * HEAD_DIM:3 * R + (hh + 1) * HEAD_DIM]
            y = d[hh] * lax.rsqrt(var[hh] + EPS) * gnw_ref[:, cols[hh]]
            gate = gh * (1.0 / (1.0 + jnp.exp(-gh)))
            mix_ref[rows, cols[hh]] = (gate * y).astype(BF16)

    while piece < PIECES_IN_RETENTION + PIECES_AFTER_RETENTION:
        project_piece(piece)
        piece += 1

    x1 = x1 + jnp.dot(mix_ref[...], wo_ref[0:R, :], preferred_element_type=F32)
    x1_ref[...] = x1
    h2 = _rms(x1, n2_ref[...])
    _store_rows(h2p_ref, (), tm, _pack_bf16_pair(h2[:, :PACKED], h2[:, PACKED:]))

    hi = h2.astype(BF16)
    lo = (h2 - hi.astype(F32)).astype(BF16)
    wrh = wrh_ref[...]
    logits = (jnp.dot(hi, wrh, preferred_element_type=F32)
              + jnp.dot(lo, wrh, preferred_element_type=F32)
              + jnp.dot(hi, wrl_ref[...], preferred_element_type=F32)
              + rb_ref[...])

    while piece < n_pieces:
        project_piece(piece)
        piece += 1

    lane = lax.broadcasted_iota(I32, (tm, LANES), 1)
    lane_f = lane.astype(F32)
    neg_inf = jnp.float32(-jnp.inf)

    def first_max(v):
        m = jnp.max(v, axis=-1, keepdims=True)
        return m, jnp.min(jnp.where(v == m, lane_f, float(LANES)), axis=-1, keepdims=True)

    gl = jnp.where(lane < N_GROUPS, logits, neg_inf)
    gmax, g_sel = first_max(gl)
    p_group = 1.0 / jnp.sum(jnp.exp(gl - gmax), axis=-1, keepdims=True)

    e_lo = EXPERT_LANE0 + g_sel * EXPERTS_PER_GROUP
    emask = (lane_f >= e_lo) & (lane_f < e_lo + EXPERTS_PER_GROUP)
    el = jnp.where(emask, logits, neg_inf)
    l1, i1 = first_max(el)
    l2, i2 = first_max(jnp.where(lane_f == i1, neg_inf, el))
    r = jnp.exp(l2 - l1)
    gate1 = p_group / (1.0 + r)
    gate2 = gate1 * r

    oh1 = jnp.where(lane_f == i1, 1.0, 0.0)
    oh2 = jnp.where(lane_f == i2, 1.0, 0.0)
    row = lax.broadcasted_iota(I32, (tm, tm), 0)
    col = lax.broadcasted_iota(I32, (tm, tm), 1)
    lower = jnp.where(row > col, 1.0, 0.0).astype(BF16)
    pre1 = jnp.dot(lower, oh1.astype(BF16), preferred_element_type=F32)
    pre2 = jnp.dot(lower, oh2.astype(BF16), preferred_element_type=F32)
    cnt1 = jnp.sum(oh1, axis=0, keepdims=True)
    cnt2 = jnp.sum(oh2, axis=0, keepdims=True)
    run = run_ref[...]
    rank1 = jnp.sum(oh1 * (pre1 + run), axis=-1, keepdims=True)
    rank2 = jnp.sum(oh2 * (pre2 + run + cnt1), axis=-1, keepdims=True)
    run_new = run + cnt1 + cnt2
    run_ref[...] = run_new
    cnt_ref[...] = run_new

    route = jnp.where(lane == 0, i1 - EXPERT_LANE0,
            jnp.where(lane == 1, i2 - EXPERT_LANE0,
            jnp.where(lane == 2, gate1,
            jnp.where(lane == 3, gate2,
            jnp.where(lane == 4, rank1,
            jnp.where(lane == 5, rank2, 0.0))))))
    route_ref[...] = route[:, :SUBLANES]
    route_t_ref[...] = jnp.transpose(route)[:SUBLANES, :]


def _mixer_router(x, n1, win_bf, cosf, sins, dmat, kdec, qdec, gnw, convw, wo_bf, n2, wrh, wrl, rb,
                  chunk_decay):
    B, S, D = x.shape
    tm = MIX_TILE
    n = B * S
    tiles_per_seq = S // tm
    n_tiles = n // tm
    x = x.reshape(n, D)
    const2 = lambda i: (0, 0)
    tile = lambda i: (i, 0)
    in_specs = [
        pl.BlockSpec((tm, D), tile),
        pl.BlockSpec((tm, D), lambda i: (jnp.minimum(i + 1, n_tiles - 1), 0)),
        pl.BlockSpec((1, D), const2),
        pl.BlockSpec((D, IN_COLS), const2),
        pl.BlockSpec((tm, HEAD_DIM), lambda i: (lax.rem(i, tiles_per_seq), 0)),
        pl.BlockSpec((tm, HEAD_DIM), lambda i: (lax.rem(i, tiles_per_seq), 0)),
        pl.BlockSpec((RET_HEADS, CHUNK, CHUNK), lambda i: (0, 0, 0)),
        pl.BlockSpec((CHUNK, RET_WIDTH), const2),
        pl.BlockSpec((CHUNK, RET_WIDTH), const2),
        pl.BlockSpec((1, RET_WIDTH), const2),
        pl.BlockSpec((CONV_K, CONV_WIDTH), const2),
        pl.BlockSpec((D, D), const2),
        pl.BlockSpec((1, D), const2),
        pl.BlockSpec((D, LANES), const2),
        pl.BlockSpec((D, LANES), const2),
        pl.BlockSpec((1, LANES), const2),
    ]
    out_specs = [
        pl.BlockSpec((tm, D), tile),
        pl.BlockSpec((_quad(tm), LANES), tile),
        pl.BlockSpec((tm, SUBLANES), tile),
        pl.BlockSpec((SUBLANES, tm), lambda i: (0, i)),
        pl.BlockSpec((1, LANES), const2),
    ]
    out_shape = [
        jax.ShapeDtypeStruct((n, D), F32),
        jax.ShapeDtypeStruct((_quad(n), LANES), U32),
        jax.ShapeDtypeStruct((n, SUBLANES), F32),
        jax.ShapeDtypeStruct((SUBLANES, n), F32),
        jax.ShapeDtypeStruct((1, LANES), F32),
    ]
    scratch = [
        pltpu.VMEM((RET_HEADS, HEAD_DIM, HEAD_DIM), F32),
        pltpu.VMEM((tm + SUBLANES, CONV_WIDTH), F32),
        pltpu.VMEM((tm, RET_WIDTH), BF16),
        pltpu.VMEM((1, LANES), F32),
        pltpu.VMEM((tm, IN_COLS), F32),
        pltpu.VMEM((tm, IN_COLS), F32),
        pltpu.VMEM((tm, D), BF16),
    ]
    return pl.pallas_call(
        functools.partial(_mixer_router_body, chunk_decay=chunk_decay, tiles_per_seq=tiles_per_seq),
        grid=(n_tiles,), in_specs=in_specs, out_specs=out_specs, out_shape=out_shape,
        scratch_shapes=scratch,
        compiler_params=pltpu.CompilerParams(
            dimension_semantics=("arbitrary",), vmem_limit_bytes=VMEM_LIMIT),
        name="mixer_router",
    )(x, x, n1, win_bf, cosf, sins, dmat, kdec, qdec, gnw, convw, wo_bf, n2, wrh, wrl, rb)


def _row_copy(src_ref, src_row, dst_ref, dst_row, sem):
    return pltpu.make_async_copy(_row_slab(src_ref, src_row), _row_slab(dst_ref, dst_row), sem)


def _zero_block_copy(zbuf, dst_ref, blk, sem):
    r0 = pl.multiple_of(blk * _quad(MOE_BLOCK), _quad(MOE_BLOCK))
    return pltpu.make_async_copy(zbuf, dst_ref.at[pl.ds(r0, _quad(MOE_BLOCK))], sem)


def _dispatch_body(dest0_ref, dest1_ref, fill_ref, h2p_ref, xs_ref, zbuf, sem, z_sem):
    tt = dest0_ref.shape[0]

    @pl.when(pl.program_id(0) == 0)
    def _():
        _zero_rows(zbuf)
        used = fill_ref[N_EXPERTS]
        total = xs_ref.shape[0] // _quad(MOE_BLOCK)

        def expert_pad(e, carry):
            blk = fill_ref[e]

            @pl.when(blk >= 0)
            def _():
                _zero_block_copy(zbuf, xs_ref, blk, z_sem).start()
            return carry

        def expert_pad_wait(e, carry):
            @pl.when(fill_ref[e] >= 0)
            def _():
                _zero_block_copy(zbuf, xs_ref, 0, z_sem).wait()
            return carry

        def tail(blk, carry):
            _zero_block_copy(zbuf, xs_ref, blk, z_sem).start()
            return carry

        def tail_wait(blk, carry):
            _zero_block_copy(zbuf, xs_ref, 0, z_sem).wait()
            return carry

        lax.fori_loop(0, N_EXPERTS, expert_pad, 0)
        lax.fori_loop(used, total, tail, 0)
        lax.fori_loop(0, N_EXPERTS, expert_pad_wait, 0)
        lax.fori_loop(used, total, tail_wait, 0)

    def issue(t, carry):
        _row_copy(h2p_ref, t, xs_ref, dest0_ref[t], sem).start(priority=0)
        _row_copy(h2p_ref, t, xs_ref, dest1_ref[t], sem).start(priority=1)
        return carry

    lax.fori_loop(0, tt, issue, 0, unroll=8)

    def drain(t, carry):
        _row_copy(h2p_ref, 0, xs_ref, 0, sem).wait()
        _row_copy(h2p_ref, 0, xs_ref, 0, sem).wait()
        return carry

    lax.fori_loop(0, tt, drain, 0, unroll=8)


def _dispatch(dest0, dest1, fill, h2p, xs_rows):
    n = dest0.shape[0]
    assert h2p.shape == (_quad(n), LANES)
    tt = ROW_TILE
    return pl.pallas_call(
        _dispatch_body,
        grid=(n // tt,),
        in_specs=[
            pl.BlockSpec((tt,), lambda i: (i,), memory_space=pltpu.SMEM),
            pl.BlockSpec((tt,), lambda i: (i,), memory_space=pltpu.SMEM),
            pl.BlockSpec(memory_space=pltpu.SMEM),
            pl.BlockSpec((_quad(tt), LANES), lambda i: (i, 0)),
        ],
        out_specs=pl.BlockSpec(memory_space=pl.ANY),
        out_shape=jax.ShapeDtypeStruct((_quad(xs_rows), LANES), U32),
        scratch_shapes=[
            pltpu.VMEM((_quad(MOE_BLOCK), LANES), U32),
            pltpu.SemaphoreType.DMA,
            pltpu.SemaphoreType.DMA,
        ],
        compiler_params=pltpu.CompilerParams(dimension_semantics=("arbitrary",)),
        name="dispatch",
    )(dest0, dest1, fill, h2p)


def _experts_body(ord_ref, row0_ref, nblk_ref, oe_ref, meta_ref, xs_ref, wg_ref, wu_ref, wd_ref, ys_ref,
                  xbuf, obuf, zbuf, wg_ring, wu_ring, wd_ring, wgb_ref, wub_ref, wdb_ref,
                  in_sem, out_sem, z_sem, w_sem):
    i = pl.program_id(0)
    n_chunks = meta_ref[0]
    n_ord = meta_ref[2]
    slot = lax.rem(i, 2)

    def weight_copies(k, ws):
        e = oe_ref[k]
        return (pltpu.make_async_copy(wg_ref.at[e], wg_ring.at[ws], w_sem.at[ws]),
                pltpu.make_async_copy(wu_ref.at[e], wu_ring.at[ws], w_sem.at[ws]),
                pltpu.make_async_copy(wd_ref.at[e], wd_ring.at[ws], w_sem.at[ws]))

    def fetch_weights(k):
        for c in weight_copies(k, lax.rem(k, WEIGHT_SLOTS)):
            c.start()

    def in_copy(chunk, s):
        r0 = pl.multiple_of(_quad(row0_ref[chunk]), _quad(MOE_BLOCK))
        return pltpu.make_async_copy(xs_ref.at[pl.ds(r0, _quad(CHUNK_ROWS))], xbuf.at[s], in_sem.at[s])

    def out_copy(chunk, s, piece):
        r0 = pl.multiple_of(_quad(row0_ref[chunk] + piece * MOE_BLOCK), _quad(MOE_BLOCK))
        return pltpu.make_async_copy(obuf.at[s, pl.ds(piece * _quad(MOE_BLOCK), _quad(MOE_BLOCK))],
                                     ys_ref.at[pl.ds(r0, _quad(MOE_BLOCK))], out_sem.at[s])

    def for_pieces(chunk, fn):
        for piece in range(CHUNK_BLOCKS):
            @pl.when(piece < nblk_ref[chunk])
            def _():
                fn(piece)

    @pl.when(i == 0)
    def _():
        in_copy(0, 0).start()
        for k in range(WEIGHT_SLOTS - 1):
            @pl.when(k < n_ord)
            def _():
                fetch_weights(k)

    @pl.when(i + 1 < n_chunks)
    def _():
        in_copy(i + 1, 1 - slot).start()

    @pl.when(i < n_chunks)
    def _():
        k = ord_ref[i]

        @pl.when((i == 0) | (k != ord_ref[jnp.maximum(i - 1, 0)]))
        def _():
            @pl.when(k + WEIGHT_SLOTS - 1 < n_ord)
            def _():
                fetch_weights(k + WEIGHT_SLOTS - 1)

            ws = lax.rem(k, WEIGHT_SLOTS)
            for c in weight_copies(k, ws):
                c.wait()
            wgb_ref[...] = wg_ring[ws].astype(BF16)
            wub_ref[...] = wu_ring[ws].astype(BF16)
            wdb_ref[...] = wd_ring[ws].astype(BF16)

        in_copy(i, slot).wait()

        @pl.when(i >= 2)
        def _():
            for_pieces(i - 2, lambda piece: out_copy(i - 2, slot, piece).wait())

        for m in range(1, CHUNK_BLOCKS + 1):
            @pl.when(nblk_ref[i] == m)
            def _():
                rows = m * MOE_BLOCK
                los, his = _load_rows(xbuf, (slot,), rows)
                xb = jnp.concatenate([v.astype(BF16) for v in los + his], axis=-1)
                g = jnp.dot(xb, wgb_ref[...], preferred_element_type=F32)
                u = jnp.dot(xb, wub_ref[...], preferred_element_type=F32)
                hmid = (g * (1.0 / (1.0 + jnp.exp(-g)))) * u
                y = jnp.dot(hmid.astype(BF16), wdb_ref[...], preferred_element_type=F32)
                packed = _pack_bf16_pair(y[:, :PACKED], y[:, PACKED:])
                _store_rows(obuf, (slot,), rows, packed)

        for_pieces(i, lambda piece: out_copy(i, slot, piece).start())

    @pl.when(i == n_chunks - 1)
    def _():
        @pl.when(i >= 1)
        def _():
            for_pieces(i - 1, lambda piece: out_copy(i - 1, 1 - slot, piece).wait())
        for_pieces(i, lambda piece: out_copy(i, slot, piece).wait())

        _zero_rows(zbuf)
        used = meta_ref[1]
        total = ys_ref.shape[0] // _quad(MOE_BLOCK)

        def z_start(blk, carry):
            _zero_block_copy(zbuf, ys_ref, blk, z_sem).start()
            return carry

        def z_wait(blk, carry):
            _zero_block_copy(zbuf, ys_ref, blk, z_sem).wait()
            return carry

        lax.fori_loop(used, total, z_start, 0)
        lax.fori_loop(used, total, z_wait, 0)


def _experts(chunk_ord, chunk_row0, chunk_nblk, ord_expert, meta, xs, n_rows_out, w_gate, w_up, w_down):
    any_space = pl.BlockSpec(memory_space=pl.ANY)
    return pl.pallas_call(
        _experts_body,
        grid_spec=pltpu.PrefetchScalarGridSpec(
            num_scalar_prefetch=5,
            grid=(chunk_ord.shape[0],),
            in_specs=[any_space, any_space, any_space, any_space],
            out_specs=any_space,
            scratch_shapes=[
                pltpu.VMEM((2, _quad(CHUNK_ROWS), LANES), U32),
                pltpu.VMEM((2, _quad(CHUNK_ROWS), LANES), U32),
                pltpu.VMEM((_quad(MOE_BLOCK), LANES), U32),
                pltpu.VMEM((WEIGHT_SLOTS, D_MODEL, EXPERT_FF), F32),
                pltpu.VMEM((WEIGHT_SLOTS, D_MODEL, EXPERT_FF), F32),
                pltpu.VMEM((WEIGHT_SLOTS, EXPERT_FF, D_MODEL), F32),
                pltpu.VMEM((D_MODEL, EXPERT_FF), BF16),
                pltpu.VMEM((D_MODEL, EXPERT_FF), BF16),
                pltpu.VMEM((EXPERT_FF, D_MODEL), BF16),
                pltpu.SemaphoreType.DMA((2,)),
                pltpu.SemaphoreType.DMA((2,)),
                pltpu.SemaphoreType.DMA,
                pltpu.SemaphoreType.DMA((WEIGHT_SLOTS,)),
            ],
        ),
        out_shape=jax.ShapeDtypeStruct((_quad(n_rows_out), LANES), U32),
        compiler_params=pltpu.CompilerParams(
            dimension_semantics=("arbitrary",), vmem_limit_bytes=VMEM_LIMIT),
        name="experts",
    )(chunk_ord, chunk_row0, chunk_nblk, ord_expert, meta, xs, w_gate, w_up, w_down)


def _combine_body(dest0_ref, dest1_ref, next0_ref, next1_ref, x1_ref, route_ref, fw_ref, ys_ref, out_ref,
                  y0_ref, y1_ref, sem):
    i = pl.program_id(0)
    tt = x1_ref.shape[0]
    slot = lax.rem(i, 2)

    def gather(d0_ref, d1_ref, s):
        def issue(t, carry):
            _row_copy(ys_ref, d0_ref[t], y0_ref.at[s], t, sem.at[s]).start(priority=0)
            _row_copy(ys_ref, d1_ref[t], y1_ref.at[s], t, sem.at[s]).start(priority=1)
            return carry

        lax.fori_loop(0, tt, issue, 0, unroll=8)

    @pl.when(i == 0)
    def _():
        gather(dest0_ref, dest1_ref, 0)

    @pl.when(i + 1 < pl.num_programs(0))
    def _():
        gather(next0_ref, next1_ref, 1 - slot)

    def drain(t, carry):
        _row_copy(ys_ref, 0, y0_ref.at[slot], 0, sem.at[slot]).wait()
        _row_copy(ys_ref, 0, y1_ref.at[slot], 0, sem.at[slot]).wait()
        return carry

    lax.fori_loop(0, tt, drain, 0, unroll=8)

    route = route_ref[...]
    g0 = route[:, 2:3]
    g1 = route[:, 3:4]
    a_lo, a_hi = (jnp.concatenate(v, axis=-1) for v in _load_rows(y0_ref, (slot,), tt))
    b_lo, b_hi = (jnp.concatenate(v, axis=-1) for v in _load_rows(y1_ref, (slot,), tt))
    x1 = x1_ref[...]
    z_lo = x1[:, :PACKED] + (a_lo * g0 + b_lo * g1)
    z_hi = x1[:, PACKED:] + (a_hi * g0 + b_hi * g1)
    ms = (jnp.sum(z_lo * z_lo, axis=-1, keepdims=True)
          + jnp.sum(z_hi * z_hi, axis=-1, keepdims=True)) * (1.0 / D_MODEL)
    inv = lax.rsqrt(ms + EPS)
    out_ref[:, :PACKED] = (z_lo * inv) * fw_ref[:, :PACKED]
    out_ref[:, PACKED:] = (z_hi * inv) * fw_ref[:, PACKED:]


def _combine(dest0, dest1, x1, route, fw, ys):
    n, d = x1.shape
    tt = ROW_TILE
    steps = n // tt
    this_tile = lambda i: (i,)
    next_tile = lambda i: (jnp.minimum(i + 1, steps - 1),)
    return pl.pallas_call(
        _combine_body,
        grid=(steps,),
        in_specs=[
            pl.BlockSpec((tt,), this_tile, memory_space=pltpu.SMEM),
            pl.BlockSpec((tt,), this_tile, memory_space=pltpu.SMEM),
            pl.BlockSpec((tt,), next_tile, memory_space=pltpu.SMEM),
            pl.BlockSpec((tt,), next_tile, memory_space=pltpu.SMEM),
            pl.BlockSpec((tt, d), lambda i: (i, 0)),
            pl.BlockSpec((tt, SUBLANES), lambda i: (i, 0)),
            pl.BlockSpec((1, d), lambda i: (0, 0)),
            pl.BlockSpec(memory_space=pl.ANY),
        ],
        out_specs=pl.BlockSpec((tt, d), lambda i: (i, 0)),
        out_shape=jax.ShapeDtypeStruct((n, d), F32),
        scratch_shapes=[
            pltpu.VMEM((2, _quad(tt), LANES), U32),
            pltpu.VMEM((2, _quad(tt), LANES), U32),
            pltpu.SemaphoreType.DMA((2,)),
        ],
        compiler_params=pltpu.CompilerParams(
            dimension_semantics=("arbitrary",), vmem_limit_bytes=VMEM_LIMIT),
        name="combine",
    )(dest0, dest1, dest0, dest1, x1, route, fw, ys)


def _retention_tables(seq):
    half = HEAD_DIM // 2
    inv = ROPE_BASE ** (-jnp.arange(half, dtype=F32) / half)
    ang = jnp.arange(seq).astype(F32)[:, None] * inv[None, :]
    cos, sin = jnp.cos(ang), jnp.sin(ang)
    cosf = jnp.concatenate([cos, cos], axis=-1)
    sins = jnp.concatenate([-sin, sin], axis=-1)
    log_g = jnp.log1p(-(2.0 ** (-5.0 - jnp.arange(RET_HEADS, dtype=F32))))
    idx = jnp.arange(CHUNK, dtype=F32)
    rel = idx[:, None] - idx[None, :]
    causal = rel >= 0
    scale = HEAD_DIM ** -0.5
    dmat = jnp.where(causal[None], jnp.exp(log_g[:, None, None] * jnp.where(causal, rel, 0.0)[None]), 0.0)
    k_decay = jnp.exp(log_g[:, None] * (CHUNK - 1 - idx)[None, :])
    q_decay = jnp.exp(log_g[:, None] * (idx + 1)[None, :])
    kdec = jnp.repeat(k_decay.T, HEAD_DIM, axis=1)
    qdec = jnp.repeat(q_decay.T, HEAD_DIM, axis=1) * scale
    return cosf, sins, dmat * scale, kdec, qdec


def _chunk_decay():
    log_g = np.log1p(-(2.0 ** (-5.0 - np.arange(RET_HEADS, dtype=np.float64))))
    return tuple(float(v) for v in np.exp(log_g * CHUNK))


def _layer(x, norm1_w, w_in, ret_gn_w, conv_w, w_o, norm2_w, router_g_w, router_g_b,
           router_e_w, router_e_b, w_gate, w_up, w_down, final_w):
    B, S, D = x.shape
    n = B * S
    cosf, sins, dmat, kdec, qdec = _retention_tables(S)

    wr = jnp.zeros((D, LANES), F32)
    wr = wr.at[:, :N_GROUPS].set(router_g_w)
    wr = wr.at[:, EXPERT_LANE0:EXPERT_LANE0 + N_EXPERTS].set(
        jnp.transpose(router_e_w, (1, 0, 2)).reshape(D, N_EXPERTS))
    wrh = wr.astype(BF16)
    wrl = (wr - wrh.astype(F32)).astype(BF16)
    rb = jnp.zeros((1, LANES), F32)
    rb = rb.at[0, :N_GROUPS].set(router_g_b)
    rb = rb.at[0, EXPERT_LANE0:EXPERT_LANE0 + N_EXPERTS].set(router_e_b.reshape(-1))

    x1, h2p, route, route_t, counts = _mixer_router(
        x, norm1_w[None], w_in.astype(BF16), cosf, sins, dmat, kdec, qdec, ret_gn_w[None], conv_w,
        w_o.astype(BF16), norm2_w[None], wrh, wrl, rb, _chunk_decay())

    expert = route_t[0:2].astype(I32)
    rank = route_t[4:6].astype(I32)
    cnt = counts[0, EXPERT_LANE0:EXPERT_LANE0 + N_EXPERTS].astype(I32)
    padded = ((cnt + MOE_BLOCK - 1) // MOE_BLOCK) * MOE_BLOCK
    pad_ends = jnp.cumsum(padded)
    pad_starts = pad_ends - padded
    eids = jnp.arange(N_EXPERTS, dtype=I32)
    start_of = jnp.sum(jnp.where(expert[:, :, None] == eids, pad_starts, 0), axis=-1)
    dest = start_of + rank
    n_blocks = -(-(n * TOP_K) // MOE_BLOCK) + N_EXPERTS
    used_blocks = pad_ends[-1] // MOE_BLOCK

    n_chunk_slots = -(-(n * TOP_K) // CHUNK_ROWS) + N_EXPERTS
    chunks_of = (padded + CHUNK_ROWS - 1) // CHUNK_ROWS
    chunk_ends = jnp.cumsum(chunks_of)
    n_chunks = chunk_ends[-1]
    ci = jnp.minimum(jnp.arange(n_chunk_slots, dtype=I32), n_chunks - 1)
    chunk_e = jnp.minimum(jnp.sum((chunk_ends[None, :] <= ci[:, None]).astype(I32), axis=1), N_EXPERTS - 1)
    onehot_e = chunk_e[:, None] == eids
    pick = lambda v: jnp.sum(jnp.where(onehot_e, v, 0), axis=-1)
    k = ci - pick(chunk_ends - chunks_of)
    blocks_e = pick(padded) // MOE_BLOCK
    first_nblk = blocks_e - (pick(chunks_of) - 1) * CHUNK_BLOCKS
    chunk_nblk = jnp.where(k == 0, first_nblk, CHUNK_BLOCKS)
    chunk_row0 = pick(pad_starts) + jnp.where(k == 0, 0, first_nblk + (k - 1) * CHUNK_BLOCKS) * MOE_BLOCK
    has_rows = padded > 0
    ord_of = jnp.cumsum(has_rows.astype(I32)) - 1
    n_ord = jnp.sum(has_rows.astype(I32))
    chunk_ord = pick(ord_of)
    ord_expert = jnp.sum(jnp.where(has_rows[None, :] & (ord_of[None, :] == eids[:, None]), eids[None, :], 0), axis=1)
    meta = jnp.stack([n_chunks, used_blocks, n_ord]).astype(I32)

    xs_rows = n_blocks * MOE_BLOCK + CHUNK_ROWS - MOE_BLOCK
    last_block = jnp.where(padded > 0, pad_ends // MOE_BLOCK - 1, -1)
    fill = jnp.concatenate([last_block, used_blocks[None]]).astype(I32)
    xs = _dispatch(dest[0], dest[1], fill, h2p, xs_rows)
    ys = _experts(chunk_ord.astype(I32), chunk_row0.astype(I32), chunk_nblk.astype(I32),
                  ord_expert.astype(I32), meta, xs, n_blocks * MOE_BLOCK, w_gate, w_up, w_down)
    out = _combine(dest[0], dest[1], x1, route, final_w[None], ys)
    return out.reshape(B, S, D)


def kernel(x, norm1_w, w_in, ret_gn_w, conv_w, w_o, norm2_w, router_g_w, router_g_b, router_e_w,
           router_e_b, w_gate, w_up, w_down, final_norm_w):
    depth = norm1_w.shape[0]
    assert depth == 1, "the final RMSNorm is fused into the last layer's combine step"
    return _layer(x, norm1_w[0], w_in[0], ret_gn_w[0], conv_w[0], w_o[0], norm2_w[0],
                  router_g_w[0], router_g_b[0], router_e_w[0], router_e_b[0],
                  w_gate[0], w_up[0], w_down[0], final_norm_w)
```

```python
import functools

import jax
import jax.numpy as jnp
import numpy as np
from jax import lax
from jax.experimental import pallas as pl
from jax.experimental.pallas import tpu as pltpu

F32 = jnp.float32
BF16 = jnp.bfloat16
U32 = jnp.uint32
I32 = jnp.int32

D_MODEL = 1024
RET_WIDTH = 512
RET_HEADS = 4
HEAD_DIM = 128
CONV_WIDTH = 512
CONV_K = 3
IN_COLS = 4 * RET_WIDTH + 3 * CONV_WIDTH
CHUNK = 128
ROPE_BASE = 10000.0
N_GROUPS = 8
EXPERTS_PER_GROUP = 8
N_EXPERTS = 64
TOP_K = 2
EXPERT_FF = 512
MOE_BLOCK = 128
EPS = 1e-6

LANES = 128
SUBLANES = 8
PACKED = D_MODEL // 2
ROW_PIECES = PACKED // LANES


def _quad(rows):
    return rows * ROW_PIECES
EXPERT_LANE0 = N_GROUPS
VMEM_LIMIT = 48 * 1024 * 1024

MIX_TILE = 256
ROW_TILE = 1024
PROJ_PIECE = 512
PIECES_IN_RETENTION = 5
PIECES_AFTER_RETENTION = 0
CHUNK_BLOCKS = 4
CHUNK_ROWS = CHUNK_BLOCKS * MOE_BLOCK
WEIGHT_SLOTS = 3


def _pack_bf16_pair(lo, hi):
    return pltpu.pack_elementwise([lo, hi], packed_dtype=BF16)


def _unpack_bf16_pair(packed):
    lo = pltpu.unpack_elementwise(packed, index=0, packed_dtype=BF16, unpacked_dtype=F32)
    hi = pltpu.unpack_elementwise(packed, index=1, packed_dtype=BF16, unpacked_dtype=F32)
    return lo, hi


def _store_rows(dst_ref, lead, rows, packed):
    for j in range(ROW_PIECES):
        dst_ref[lead + (pl.ds(j, rows, stride=ROW_PIECES), slice(None))] = packed[:, j * LANES:(j + 1) * LANES]


def _load_rows(src_ref, lead, rows):
    los, his = [], []
    for j in range(ROW_PIECES):
        lo, hi = _unpack_bf16_pair(src_ref[lead + (pl.ds(j, rows, stride=ROW_PIECES), slice(None))])
        los.append(lo)
        his.append(hi)
    return los, his


def _row_slab(ref, row):
    return ref.at[pl.ds(pl.multiple_of(row * ROW_PIECES, ROW_PIECES), ROW_PIECES)]


def _zero_rows(zbuf):
    zero = jnp.zeros(zbuf.shape, F32)
    zbuf[...] = _pack_bf16_pair(zero, zero)


def _rms(x, w):
    ms = jnp.mean(x * x, axis=-1, keepdims=True)
    return (x * lax.rsqrt(ms + EPS)) * w


def _in_projection(x_ref, n1_ref, win_ref, proj_ref):
    h = _rms(x_ref[...], n1_ref[...])
    proj_ref[...] = jnp.dot(h.astype(BF16), win_ref[...], preferred_element_type=F32)


def _mixer_router_body(x_ref, xn_ref, n1_ref, win_ref, cos_ref, sin_ref, dmat_ref, kdec_ref, qdec_ref,
                       gnw_ref, convw_ref, wo_ref, n2_ref, wr2_ref, rb_ref,
                       x1_ref, h2p_ref, route_ref, route_t_ref, cnt_ref,
                       state_ref, cu_ref, mix_ref, run_ref, proj_a_ref, proj_b_ref, hn_ref,
                       *, chunk_decay, tiles_per_seq):
    i = pl.program_id(0)
    c = lax.rem(i, tiles_per_seq)

    @pl.when(c == 0)
    def _():
        state_ref[...] = jnp.zeros_like(state_ref)
        cu_ref[0:SUBLANES, :] = jnp.zeros((SUBLANES, CONV_WIDTH), F32)

    @pl.when(i == 0)
    def _():
        run_ref[...] = jnp.zeros_like(run_ref)
        _in_projection(x_ref, n1_ref, win_ref, proj_a_ref)

    def step(proj_ref, proj_next_ref):
        hn_ref[...] = _rms(xn_ref[...], n1_ref[...]).astype(BF16)

        def project_piece(p):
            cols = slice(p * PROJ_PIECE, (p + 1) * PROJ_PIECE)
            proj_next_ref[:, cols] = jnp.dot(hn_ref[...], win_ref[:, cols], preferred_element_type=F32)

        _mix_and_route(proj_ref, x_ref, cos_ref, sin_ref, dmat_ref, kdec_ref, qdec_ref, gnw_ref,
                       convw_ref, wo_ref, n2_ref, wr2_ref, rb_ref,
                       x1_ref, h2p_ref, route_ref, route_t_ref, cnt_ref,
                       state_ref, cu_ref, mix_ref, run_ref, chunk_decay, project_piece)

    @pl.when(lax.rem(i, 2) == 0)
    def _():
        step(proj_a_ref, proj_b_ref)

    @pl.when(lax.rem(i, 2) == 1)
    def _():
        step(proj_b_ref, proj_a_ref)


def _mix_and_route(proj, x_ref, cos_ref, sin_ref, dmat_ref, kdec_ref, qdec_ref, gnw_ref,
                   convw_ref, wo_ref, n2_ref, wr2_ref, rb_ref,
                   x1_ref, h2p_ref, route_ref, route_t_ref, cnt_ref,
                   state_ref, cu_ref, mix_ref, run_ref, chunk_decay, project_piece):
    tm = x_ref.shape[0]
    x = x_ref[...]
    R = RET_WIDTH
    cosf = cos_ref[...]
    sins = sin_ref[...]
    n_pieces = IN_COLS // PROJ_PIECE
    piece = 0

    gb = proj[:, 4 * R:4 * R + CONV_WIDTH]
    gc = proj[:, 4 * R + CONV_WIDTH:4 * R + 2 * CONV_WIDTH]
    u = proj[:, 4 * R + 2 * CONV_WIDTH:4 * R + 3 * CONV_WIDTH]
    cu_ref[SUBLANES:SUBLANES + tm, :] = gc * u
    conv = (convw_ref[2:3, :] * cu_ref[SUBLANES:SUBLANES + tm, :]
            + convw_ref[1:2, :] * cu_ref[SUBLANES - 1:SUBLANES - 1 + tm, :]
            + convw_ref[0:1, :] * cu_ref[SUBLANES - 2:SUBLANES - 2 + tm, :])
    cu_ref[0:SUBLANES, :] = cu_ref[tm:tm + SUBLANES, :]
    x1 = x + jnp.dot((gb * conv).astype(BF16), wo_ref[R:R + CONV_WIDTH, :], preferred_element_type=F32)

    heads = range(RET_HEADS)

    def next_piece():
        nonlocal piece
        if piece < PIECES_IN_RETENTION:
            project_piece(piece)
            piece += 1

    for j in range(tm // CHUNK):
        rows = slice(j * CHUNK, (j + 1) * CHUNK)
        cols = [slice(hh * HEAD_DIM, (hh + 1) * HEAD_DIM) for hh in heads]
        cs = cosf[rows, :]
        sn = sins[rows, :]
        qr, kr, vb = [], [], []
        for hh in heads:
            c0 = hh * HEAD_DIM
            qh = proj[rows, c0:c0 + HEAD_DIM]
            kh = proj[rows, R + c0:R + c0 + HEAD_DIM]
            qr.append(qh * cs + pltpu.roll(qh, HEAD_DIM // 2, 1) * sn)
            kr.append(kh * cs + pltpu.roll(kh, HEAD_DIM // 2, 1) * sn)
            vb.append(proj[rows, 2 * R + c0:2 * R + c0 + HEAD_DIM].astype(BF16))
        next_piece()
        scores = [lax.dot_general(qr[hh].astype(BF16), kr[hh].astype(BF16), (((1,), (1,)), ((), ())),
                                  preferred_element_type=F32) for hh in heads]
        s_prev = [state_ref[hh] for hh in heads]
        inter = [jnp.dot((qr[hh] * qdec_ref[:, cols[hh]]).astype(BF16), s_prev[hh].astype(BF16),
                         preferred_element_type=F32) for hh in heads]
        next_piece()
        kv = [lax.dot_general((kr[hh] * kdec_ref[:, cols[hh]]).astype(BF16), vb[hh],
                              (((0,), (0,)), ((), ())), preferred_element_type=F32) for hh in heads]
        for hh in heads:
            state_ref[hh] = chunk_decay[hh] * s_prev[hh] + kv[hh]
        o = [jnp.dot((scores[hh] * dmat_ref[hh]).astype(BF16), vb[hh], preferred_element_type=F32)
             + inter[hh] for hh in heads]
        next_piece()
        mu = [jnp.mean(o[hh], axis=-1, keepdims=True) for hh in heads]
        d = [o[hh] - mu[hh] for hh in heads]
        var = [jnp.mean(d[hh] * d[hh], axis=-1, keepdims=True) for hh in heads]
        for hh in heads:
            gh = proj[rows, 3 * R + hh * HEAD_DIM:3 * R + (hh + 1) * HEAD_DIM]
            y = d[hh] * lax.rsqrt(var[hh] + EPS) * gnw_ref[:, cols[hh]]
            gate = gh * (1.0 / (1.0 + jnp.exp(-gh)))
            mix_ref[rows, cols[hh]] = (gate * y).astype(BF16)

    while piece < PIECES_IN_RETENTION + PIECES_AFTER_RETENTION:
        project_piece(piece)
        piece += 1

    x1 = x1 + jnp.dot(mix_ref[...], wo_ref[0:R, :], preferred_element_type=F32)
    x1_ref[...] = x1
    h2 = _rms(x1, n2_ref[...])
    _store_rows(h2p_ref, (), tm, _pack_bf16_pair(h2[:, :PACKED], h2[:, PACKED:]))

    hi = h2.astype(BF16)
    lo = (h2 - hi.astype(F32)).astype(BF16)
    wrh = wr2_ref[:, 0:LANES]
    logits = (jnp.dot(hi, wrh, preferred_element_type=F32)
              + jnp.dot(lo, wrh, preferred_element_type=F32)
              + jnp.dot(hi, wr2_ref[:, LANES:2 * LANES], preferred_element_type=F32)
              + rb_ref[...])

    while piece < n_pieces:
        project_piece(piece)
        piece += 1

    lane = lax.broadcasted_iota(I32, (tm, LANES), 1)
    lane_f = lane.astype(F32)
    neg_inf = jnp.float32(-jnp.inf)

    def first_max(v):
        m = jnp.max(v, axis=-1, keepdims=True)
        return m, jnp.min(jnp.where(v == m, lane_f, float(LANES)), axis=-1, keepdims=True)

    gl = jnp.where(lane < N_GROUPS, logits, neg_inf)
    gmax, g_sel = first_max(gl)
    p_group = 1.0 / jnp.sum(jnp.exp(gl - gmax), axis=-1, keepdims=True)

    e_lo = EXPERT_LANE0 + g_sel * EXPERTS_PER_GROUP
    emask = (lane_f >= e_lo) & (lane_f < e_lo + EXPERTS_PER_GROUP)
    el = jnp.where(emask, logits, neg_inf)
    l1, i1 = first_max(el)
    l2, i2 = first_max(jnp.where(lane_f == i1, neg_inf, el))
    r = jnp.exp(l2 - l1)
    gate1 = p_group / (1.0 + r)
    gate2 = gate1 * r

    oh1 = jnp.where(lane_f == i1, 1.0, 0.0)
    oh2 = jnp.where(lane_f == i2, 1.0, 0.0)
    row = lax.broadcasted_iota(I32, (tm, tm), 0)
    col = lax.broadcasted_iota(I32, (tm, tm), 1)
    lower = jnp.where(row > col, 1.0, 0.0).astype(BF16)
    pre1 = jnp.dot(lower, oh1.astype(BF16), preferred_element_type=F32)
    pre2 = jnp.dot(lower, oh2.astype(BF16), preferred_element_type=F32)
    cnt1 = jnp.sum(oh1, axis=0, keepdims=True)
    cnt2 = jnp.sum(oh2, axis=0, keepdims=True)
    run = run_ref[...]
    rank1 = jnp.sum(oh1 * (pre1 + run), axis=-1, keepdims=True)
    rank2 = jnp.sum(oh2 * (pre2 + run + cnt1), axis=-1, keepdims=True)
    run_new = run + cnt1 + cnt2
    run_ref[...] = run_new
    cnt_ref[...] = run_new

    route = jnp.where(lane == 0, i1 - EXPERT_LANE0,
            jnp.where(lane == 1, i2 - EXPERT_LANE0,
            jnp.where(lane == 2, gate1,
            jnp.where(lane == 3, gate2,
            jnp.where(lane == 4, rank1,
            jnp.where(lane == 5, rank2, 0.0))))))
    route_ref[...] = route[:, :SUBLANES]
    route_t_ref[...] = jnp.transpose(route)[:SUBLANES, :]


def _mixer_router(x, n1, win_bf, cosf, sins, dmat, kdec, qdec, gnw, convw, wo_bf, n2, wr2, rb,
                  chunk_decay):
    B, S, D = x.shape
    tm = MIX_TILE
    n = B * S
    tiles_per_seq = S // tm
    n_tiles = n // tm
    x = x.reshape(n, D)
    const2 = lambda i: (0, 0)
    tile = lambda i: (i, 0)
    in_specs = [
        pl.BlockSpec((tm, D), tile),
        pl.BlockSpec((tm, D), lambda i: (jnp.minimum(i + 1, n_tiles - 1), 0)),
        pl.BlockSpec((1, D), const2),
        pl.BlockSpec((D, IN_COLS), const2),
        pl.BlockSpec((tm, HEAD_DIM), lambda i: (lax.rem(i, tiles_per_seq), 0)),
        pl.BlockSpec((tm, HEAD_DIM), lambda i: (lax.rem(i, tiles_per_seq), 0)),
        pl.BlockSpec((RET_HEADS, CHUNK, CHUNK), lambda i: (0, 0, 0)),
        pl.BlockSpec((CHUNK, RET_WIDTH), const2),
        pl.BlockSpec((CHUNK, RET_WIDTH), const2),
        pl.BlockSpec((1, RET_WIDTH), const2),
        pl.BlockSpec((CONV_K, CONV_WIDTH), const2),
        pl.BlockSpec((D, D), const2),
        pl.BlockSpec((1, D), const2),
        pl.BlockSpec((D, 2 * LANES), const2),
        pl.BlockSpec((1, LANES), const2),
    ]
    out_specs = [
        pl.BlockSpec((tm, D), tile),
        pl.BlockSpec((_quad(tm), LANES), tile),
        pl.BlockSpec((tm, SUBLANES), tile),
        pl.BlockSpec((SUBLANES, tm), lambda i: (0, i)),
        pl.BlockSpec((1, LANES), const2),
    ]
    out_shape = [
        jax.ShapeDtypeStruct((n, D), F32),
        jax.ShapeDtypeStruct((_quad(n), LANES), U32),
        jax.ShapeDtypeStruct((n, SUBLANES), F32),
        jax.ShapeDtypeStruct((SUBLANES, n), F32),
        jax.ShapeDtypeStruct((1, LANES), F32),
    ]
    scratch = [
        pltpu.VMEM((RET_HEADS, HEAD_DIM, HEAD_DIM), F32),
        pltpu.VMEM((tm + SUBLANES, CONV_WIDTH), F32),
        pltpu.VMEM((tm, RET_WIDTH), BF16),
        pltpu.VMEM((1, LANES), F32),
        pltpu.VMEM((tm, IN_COLS), F32),
        pltpu.VMEM((tm, IN_COLS), F32),
        pltpu.VMEM((tm, D), BF16),
    ]
    return pl.pallas_call(
        functools.partial(_mixer_router_body, chunk_decay=chunk_decay, tiles_per_seq=tiles_per_seq),
        grid=(n_tiles,), in_specs=in_specs, out_specs=out_specs, out_shape=out_shape,
        scratch_shapes=scratch,
        compiler_params=pltpu.CompilerParams(
            dimension_semantics=("arbitrary",), vmem_limit_bytes=VMEM_LIMIT),
        name="mixer_router",
    )(x, x, n1, win_bf, cosf, sins, dmat, kdec, qdec, gnw, convw, wo_bf, n2, wr2, rb)


def _row_copy(src_ref, src_row, dst_ref, dst_row, sem):
    return pltpu.make_async_copy(_row_slab(src_ref, src_row), _row_slab(dst_ref, dst_row), sem)


def _zero_block_copy(zbuf, dst_ref, blk, sem):
    r0 = pl.multiple_of(blk * _quad(MOE_BLOCK), _quad(MOE_BLOCK))
    return pltpu.make_async_copy(zbuf, dst_ref.at[pl.ds(r0, _quad(MOE_BLOCK))], sem)


def _dispatch_body(dest0_ref, dest1_ref, fill_ref, h2p_ref, xs_ref, zbuf, sem, z_sem):
    tt = dest0_ref.shape[0]

    @pl.when(pl.program_id(0) == 0)
    def _():
        _zero_rows(zbuf)
        used = fill_ref[N_EXPERTS]
        total = xs_ref.shape[0] // _quad(MOE_BLOCK)

        def expert_pad(e, carry):
            blk = fill_ref[e]

            @pl.when(blk >= 0)
            def _():
                _zero_block_copy(zbuf, xs_ref, blk, z_sem).start()
            return carry

        def expert_pad_wait(e, carry):
            @pl.when(fill_ref[e] >= 0)
            def _():
                _zero_block_copy(zbuf, xs_ref, 0, z_sem).wait()
            return carry

        def tail(blk, carry):
            _zero_block_copy(zbuf, xs_ref, blk, z_sem).start()
            return carry

        def tail_wait(blk, carry):
            _zero_block_copy(zbuf, xs_ref, 0, z_sem).wait()
            return carry

        lax.fori_loop(0, N_EXPERTS, expert_pad, 0)
        lax.fori_loop(used, total, tail, 0)
        lax.fori_loop(0, N_EXPERTS, expert_pad_wait, 0)
        lax.fori_loop(used, total, tail_wait, 0)

    def issue(t, carry):
        _row_copy(h2p_ref, t, xs_ref, dest0_ref[t], sem).start(priority=0)
        _row_copy(h2p_ref, t, xs_ref, dest1_ref[t], sem).start(priority=1)
        return carry

    lax.fori_loop(0, tt, issue, 0, unroll=8)

    def drain(t, carry):
        _row_copy(h2p_ref, 0, xs_ref, 0, sem).wait()
        _row_copy(h2p_ref, 0, xs_ref, 0, sem).wait()
        return carry

    lax.fori_loop(0, tt, drain, 0, unroll=8)


def _dispatch(dest0, dest1, fill, h2p, xs_rows):
    n = dest0.shape[0]
    assert h2p.shape == (_quad(n), LANES)
    tt = ROW_TILE
    return pl.pallas_call(
        _dispatch_body,
        grid=(n // tt,),
        in_specs=[
            pl.BlockSpec((tt,), lambda i: (i,), memory_space=pltpu.SMEM),
            pl.BlockSpec((tt,), lambda i: (i,), memory_space=pltpu.SMEM),
            pl.BlockSpec(memory_space=pltpu.SMEM),
            pl.BlockSpec((_quad(tt), LANES), lambda i: (i, 0)),
        ],
        out_specs=pl.BlockSpec(memory_space=pl.ANY),
        out_shape=jax.ShapeDtypeStruct((_quad(xs_rows), LANES), U32),
        scratch_shapes=[
            pltpu.VMEM((_quad(MOE_BLOCK), LANES), U32),
            pltpu.SemaphoreType.DMA,
            pltpu.SemaphoreType.DMA,
        ],
        compiler_params=pltpu.CompilerParams(dimension_semantics=("arbitrary",)),
        name="dispatch",
    )(dest0, dest1, fill, h2p)


def _experts_body(ord_ref, row0_ref, nblk_ref, oe_ref, meta_ref, xs_ref, wg_ref, wu_ref, wd_ref, ys_ref,
                  xbuf, obuf, zbuf, wg_ring, wu_ring, wd_ring, wgb_ref, wub_ref, wdb_ref,
                  in_sem, out_sem, z_sem, w_sem):
    i = pl.program_id(0)
    n_chunks = meta_ref[0]
    n_ord = meta_ref[2]
    slot = lax.rem(i, 2)

    def weight_copies(k, ws):
        e = oe_ref[k]
        return (pltpu.make_async_copy(wg_ref.at[e], wg_ring.at[ws], w_sem.at[ws]),
                pltpu.make_async_copy(wu_ref.at[e], wu_ring.at[ws], w_sem.at[ws]),
                pltpu.make_async_copy(wd_ref.at[e], wd_ring.at[ws], w_sem.at[ws]))

    def fetch_weights(k):
        for c in weight_copies(k, lax.rem(k, WEIGHT_SLOTS)):
            c.start()

    def in_copy(chunk, s, piece):
        r0 = pl.multiple_of(_quad(row0_ref[chunk] + piece * MOE_BLOCK), _quad(MOE_BLOCK))
        return pltpu.make_async_copy(xs_ref.at[pl.ds(r0, _quad(MOE_BLOCK))],
                                     xbuf.at[s, pl.ds(piece * _quad(MOE_BLOCK), _quad(MOE_BLOCK))], in_sem.at[s])

    def out_copy(chunk, s, piece):
        r0 = pl.multiple_of(_quad(row0_ref[chunk] + piece * MOE_BLOCK), _quad(MOE_BLOCK))
        return pltpu.make_async_copy(obuf.at[s, pl.ds(piece * _quad(MOE_BLOCK), _quad(MOE_BLOCK))],
                                     ys_ref.at[pl.ds(r0, _quad(MOE_BLOCK))], out_sem.at[s])

    def for_pieces(chunk, fn):
        for piece in range(CHUNK_BLOCKS):
            @pl.when(piece < nblk_ref[chunk])
            def _():
                fn(piece)

    @pl.when(i == 0)
    def _():
        for_pieces(0, lambda piece: in_copy(0, 0, piece).start())
        for k in range(WEIGHT_SLOTS - 1):
            @pl.when(k < n_ord)
            def _():
                fetch_weights(k)

    @pl.when(i + 1 < n_chunks)
    def _():
        for_pieces(i + 1, lambda piece: in_copy(i + 1, 1 - slot, piece).start())

    @pl.when(i < n_chunks)
    def _():
        k = ord_ref[i]

        @pl.when((i == 0) | (k != ord_ref[jnp.maximum(i - 1, 0)]))
        def _():
            @pl.when(k + WEIGHT_SLOTS - 1 < n_ord)
            def _():
                fetch_weights(k + WEIGHT_SLOTS - 1)

            ws = lax.rem(k, WEIGHT_SLOTS)
            for c in weight_copies(k, ws):
                c.wait()
            wgb_ref[...] = wg_ring[ws].astype(BF16)
            wub_ref[...] = wu_ring[ws].astype(BF16)
            wdb_ref[...] = wd_ring[ws].astype(BF16)

        for_pieces(i, lambda piece: in_copy(i, slot, piece).wait())

        @pl.when(i >= 2)
        def _():
            for_pieces(i - 2, lambda piece: out_copy(i - 2, slot, piece).wait())

        for m in range(1, CHUNK_BLOCKS + 1):
            @pl.when(nblk_ref[i] == m)
            def _():
                rows = m * MOE_BLOCK
                los, his = _load_rows(xbuf, (slot,), rows)
                xb = jnp.concatenate([v.astype(BF16) for v in los + his], axis=-1)
                g = jnp.dot(xb, wgb_ref[...], preferred_element_type=F32)
                u = jnp.dot(xb, wub_ref[...], preferred_element_type=F32)
                hmid = (g * (1.0 / (1.0 + jnp.exp(-g)))) * u
                y = jnp.dot(hmid.astype(BF16), wdb_ref[...], preferred_element_type=F32)
                packed = _pack_bf16_pair(y[:, :PACKED], y[:, PACKED:])
                _store_rows(obuf, (slot,), rows, packed)

        for_pieces(i, lambda piece: out_copy(i, slot, piece).start())

    @pl.when(i == n_chunks - 1)
    def _():
        @pl.when(i >= 1)
        def _():
            for_pieces(i - 1, lambda piece: out_copy(i - 1, 1 - slot, piece).wait())
        for_pieces(i, lambda piece: out_copy(i, slot, piece).wait())

        _zero_rows(zbuf)
        used = meta_ref[1]
        total = ys_ref.shape[0] // _quad(MOE_BLOCK)

        def z_start(blk, carry):
            _zero_block_copy(zbuf, ys_ref, blk, z_sem).start()
            return carry

        def z_wait(blk, carry):
            _zero_block_copy(zbuf, ys_ref, blk, z_sem).wait()
            return carry

        lax.fori_loop(used, total, z_start, 0)
        lax.fori_loop(used, total, z_wait, 0)


def _experts(chunk_ord, chunk_row0, chunk_nblk, ord_expert, meta, xs, n_rows_out, w_gate, w_up, w_down):
    any_space = pl.BlockSpec(memory_space=pl.ANY)
    return pl.pallas_call(
        _experts_body,
        grid_spec=pltpu.PrefetchScalarGridSpec(
            num_scalar_prefetch=5,
            grid=(chunk_ord.shape[0],),
            in_specs=[any_space, any_space, any_space, any_space],
            out_specs=any_space,
            scratch_shapes=[
                pltpu.VMEM((2, _quad(CHUNK_ROWS), LANES), U32),
                pltpu.VMEM((2, _quad(CHUNK_ROWS), LANES), U32),
                pltpu.VMEM((_quad(MOE_BLOCK), LANES), U32),
                pltpu.VMEM((WEIGHT_SLOTS, D_MODEL, EXPERT_FF), F32),
                pltpu.VMEM((WEIGHT_SLOTS, D_MODEL, EXPERT_FF), F32),
                pltpu.VMEM((WEIGHT_SLOTS, EXPERT_FF, D_MODEL), F32),
                pltpu.VMEM((D_MODEL, EXPERT_FF), BF16),
                pltpu.VMEM((D_MODEL, EXPERT_FF), BF16),
                pltpu.VMEM((EXPERT_FF, D_MODEL), BF16),
                pltpu.SemaphoreType.DMA((2,)),
                pltpu.SemaphoreType.DMA((2,)),
                pltpu.SemaphoreType.DMA,
                pltpu.SemaphoreType.DMA((WEIGHT_SLOTS,)),
            ],
        ),
        out_shape=jax.ShapeDtypeStruct((_quad(n_rows_out), LANES), U32),
        compiler_params=pltpu.CompilerParams(
            dimension_semantics=("arbitrary",), vmem_limit_bytes=VMEM_LIMIT),
        name="experts",
    )(chunk_ord, chunk_row0, chunk_nblk, ord_expert, meta, xs, w_gate, w_up, w_down)


def _combine_body(dest0_ref, dest1_ref, next0_ref, next1_ref, x1_ref, route_ref, fw_ref, ys_ref, out_ref,
                  y0_ref, y1_ref, sem):
    i = pl.program_id(0)
    tt = x1_ref.shape[0]
    slot = lax.rem(i, 2)

    def gather(d0_ref, d1_ref, s):
        def issue(t, carry):
            _row_copy(ys_ref, d0_ref[t], y0_ref.at[s], t, sem.at[s]).start(priority=0)
            _row_copy(ys_ref, d1_ref[t], y1_ref.at[s], t, sem.at[s]).start(priority=1)
            return carry

        lax.fori_loop(0, tt, issue, 0, unroll=8)

    @pl.when(i == 0)
    def _():
        gather(dest0_ref, dest1_ref, 0)

    @pl.when(i + 1 < pl.num_programs(0))
    def _():
        gather(next0_ref, next1_ref, 1 - slot)

    def drain(t, carry):
        _row_copy(ys_ref, 0, y0_ref.at[slot], 0, sem.at[slot]).wait()
        _row_copy(ys_ref, 0, y1_ref.at[slot], 0, sem.at[slot]).wait()
        return carry

    lax.fori_loop(0, tt, drain, 0, unroll=8)

    route = route_ref[...]
    g0 = route[:, 2:3]
    g1 = route[:, 3:4]
    a_lo, a_hi = (jnp.concatenate(v, axis=-1) for v in _load_rows(y0_ref, (slot,), tt))
    b_lo, b_hi = (jnp.concatenate(v, axis=-1) for v in _load_rows(y1_ref, (slot,), tt))
    x1 = x1_ref[...]
    z_lo = x1[:, :PACKED] + (a_lo * g0 + b_lo * g1)
    z_hi = x1[:, PACKED:] + (a_hi * g0 + b_hi * g1)
    ms = (jnp.sum(z_lo * z_lo, axis=-1, keepdims=True)
          + jnp.sum(z_hi * z_hi, axis=-1, keepdims=True)) * (1.0 / D_MODEL)
    inv = lax.rsqrt(ms + EPS)
    out_ref[:, :PACKED] = (z_lo * inv) * fw_ref[:, :PACKED]
    out_ref[:, PACKED:] = (z_hi * inv) * fw_ref[:, PACKED:]


def _combine(dest0, dest1, x1, route, fw, ys):
    n, d = x1.shape
    tt = ROW_TILE
    steps = n // tt
    this_tile = lambda i: (i,)
    next_tile = lambda i: (jnp.minimum(i + 1, steps - 1),)
    return pl.pallas_call(
        _combine_body,
        grid=(steps,),
        in_specs=[
            pl.BlockSpec((tt,), this_tile, memory_space=pltpu.SMEM),
            pl.BlockSpec((tt,), this_tile, memory_space=pltpu.SMEM),
            pl.BlockSpec((tt,), next_tile, memory_space=pltpu.SMEM),
            pl.BlockSpec((tt,), next_tile, memory_space=pltpu.SMEM),
            pl.BlockSpec((tt, d), lambda i: (i, 0)),
            pl.BlockSpec((tt, SUBLANES), lambda i: (i, 0)),
            pl.BlockSpec((1, d), lambda i: (0, 0)),
            pl.BlockSpec(memory_space=pl.ANY),
        ],
        out_specs=pl.BlockSpec((tt, d), lambda i: (i, 0)),
        out_shape=jax.ShapeDtypeStruct((n, d), F32),
        scratch_shapes=[
            pltpu.VMEM((2, _quad(tt), LANES), U32),
            pltpu.VMEM((2, _quad(tt), LANES), U32),
            pltpu.SemaphoreType.DMA((2,)),
        ],
        compiler_params=pltpu.CompilerParams(
            dimension_semantics=("arbitrary",), vmem_limit_bytes=VMEM_LIMIT),
        name="combine",
    )(dest0, dest1, dest0, dest1, x1, route, fw, ys)


def _retention_tables(seq):
    half = HEAD_DIM // 2
    inv = ROPE_BASE ** (-jnp.arange(half, dtype=F32) / half)
    ang = jnp.arange(seq).astype(F32)[:, None] * inv[None, :]
    cos, sin = jnp.cos(ang), jnp.sin(ang)
    cosf = jnp.concatenate([cos, cos], axis=-1)
    sins = jnp.concatenate([-sin, sin], axis=-1)
    log_g = jnp.log1p(-(2.0 ** (-5.0 - jnp.arange(RET_HEADS, dtype=F32))))
    idx = jnp.arange(CHUNK, dtype=F32)
    rel = idx[:, None] - idx[None, :]
    causal = rel >= 0
    scale = HEAD_DIM ** -0.5
    dmat = jnp.where(causal[None], jnp.exp(log_g[:, None, None] * jnp.where(causal, rel, 0.0)[None]), 0.0)
    k_decay = jnp.exp(log_g[:, None] * (CHUNK - 1 - idx)[None, :])
    q_decay = jnp.exp(log_g[:, None] * (idx + 1)[None, :])
    kdec = jnp.repeat(k_decay.T, HEAD_DIM, axis=1)
    qdec = jnp.repeat(q_decay.T, HEAD_DIM, axis=1) * scale
    return cosf, sins, dmat * scale, kdec, qdec


def _chunk_decay():
    log_g = np.log1p(-(2.0 ** (-5.0 - np.arange(RET_HEADS, dtype=np.float64))))
    return tuple(float(v) for v in np.exp(log_g * CHUNK))


def _layer(x, norm1_w, w_in, ret_gn_w, conv_w, w_o, norm2_w, router_g_w, router_g_b,
           router_e_w, router_e_b, w_gate, w_up, w_down, final_w):
    B, S, D = x.shape
    n = B * S
    cosf, sins, dmat, kdec, qdec = _retention_tables(S)

    wr = jnp.zeros((D, LANES), F32)
    wr = wr.at[:, :N_GROUPS].set(router_g_w)
    wr = wr.at[:, EXPERT_LANE0:EXPERT_LANE0 + N_EXPERTS].set(
        jnp.transpose(router_e_w, (1, 0, 2)).reshape(D, N_EXPERTS))
    wrh = wr.astype(BF16)
    wrl = (wr - wrh.astype(F32)).astype(BF16)
    wr2 = jnp.concatenate([wrh, wrl], axis=1)
    rb = jnp.zeros((1, LANES), F32)
    rb = rb.at[0, :N_GROUPS].set(router_g_b)
    rb = rb.at[0, EXPERT_LANE0:EXPERT_LANE0 + N_EXPERTS].set(router_e_b.reshape(-1))

    x1, h2p, route, route_t, counts = _mixer_router(
        x, norm1_w[None], w_in.astype(BF16), cosf, sins, dmat, kdec, qdec, ret_gn_w[None], conv_w,
        w_o.astype(BF16), norm2_w[None], wr2, rb, _chunk_decay())

    expert = route_t[0:2].astype(I32)
    rank = route_t[4:6].astype(I32)
    cnt = counts[0, EXPERT_LANE0:EXPERT_LANE0 + N_EXPERTS].astype(I32)
    padded = ((cnt + MOE_BLOCK - 1) // MOE_BLOCK) * MOE_BLOCK
    pad_ends = jnp.cumsum(padded)
    pad_starts = pad_ends - padded
    eids = jnp.arange(N_EXPERTS, dtype=I32)
    start_of = jnp.sum(jnp.where(expert[:, :, None] == eids, pad_starts, 0), axis=-1)
    dest = start_of + rank
    n_blocks = -(-(n * TOP_K) // MOE_BLOCK) + N_EXPERTS
    used_blocks = pad_ends[-1] // MOE_BLOCK

    n_chunk_slots = -(-(n * TOP_K) // CHUNK_ROWS) + N_EXPERTS
    chunks_of = (padded + CHUNK_ROWS - 1) // CHUNK_ROWS
    chunk_ends = jnp.cumsum(chunks_of)
    n_chunks = chunk_ends[-1]
    ci = jnp.minimum(jnp.arange(n_chunk_slots, dtype=I32), n_chunks - 1)
    chunk_e = jnp.minimum(jnp.sum((chunk_ends[None, :] <= ci[:, None]).astype(I32), axis=1), N_EXPERTS - 1)
    onehot_e = chunk_e[:, None] == eids
    pick = lambda v: jnp.sum(jnp.where(onehot_e, v, 0), axis=-1)
    k = ci - pick(chunk_ends - chunks_of)
    blocks_e = pick(padded) // MOE_BLOCK
    first_nblk = blocks_e - (pick(chunks_of) - 1) * CHUNK_BLOCKS
    chunk_nblk = jnp.where(k == 0, first_nblk, CHUNK_BLOCKS)
    chunk_row0 = pick(pad_starts) + jnp.where(k == 0, 0, first_nblk + (k - 1) * CHUNK_BLOCKS) * MOE_BLOCK
    has_rows = padded > 0
    ord_of = jnp.cumsum(has_rows.astype(I32)) - 1
    n_ord = jnp.sum(has_rows.astype(I32))
    chunk_ord = pick(ord_of)
    ord_expert = jnp.sum(jnp.where(has_rows[None, :] & (ord_of[None, :] == eids[:, None]), eids[None, :], 0), axis=1)
    meta = jnp.stack([n_chunks, used_blocks, n_ord]).astype(I32)

    xs_rows = n_blocks * MOE_BLOCK + CHUNK_ROWS - MOE_BLOCK
    last_block = jnp.where(padded > 0, pad_ends // MOE_BLOCK - 1, -1)
    fill = jnp.concatenate([last_block, used_blocks[None]]).astype(I32)
    xs = _dispatch(dest[0], dest[1], fill, h2p, xs_rows)
    ys = _experts(chunk_ord.astype(I32), chunk_row0.astype(I32), chunk_nblk.astype(I32),
                  ord_expert.astype(I32), meta, xs, n_blocks * MOE_BLOCK, w_gate, w_up, w_down)
    out = _combine(dest[0], dest[1], x1, route, final_w[None], ys)
    return out.reshape(B, S, D)


def kernel(x, norm1_w, w_in, ret_gn_w, conv_w, w_o, norm2_w, router_g_w, router_g_b, router_e_w,
           router_e_b, w_gate, w_up, w_down, final_norm_w):
    depth = norm1_w.shape[0]
    assert depth == 1, "the final RMSNorm is fused into the last layer's combine step"
    return _layer(x, norm1_w[0], w_in[0], ret_gn_w[0], conv_w[0], w_o[0], norm2_w[0],
                  router_g_w[0], router_g_b[0], router_e_w[0], router_e_b[0],
                  w_gate[0], w_up[0], w_down[0], final_norm_w)
```

```python
import functools

import jax
import jax.numpy as jnp
import numpy as np
from jax import lax
from jax.experimental import pallas as pl
from jax.experimental.pallas import tpu as pltpu

F32 = jnp.float32
BF16 = jnp.bfloat16
U32 = jnp.uint32
I32 = jnp.int32

D_MODEL = 1024
RET_WIDTH = 512
RET_HEADS = 4
HEAD_DIM = 128
CONV_WIDTH = 512
CONV_K = 3
IN_COLS = 4 * RET_WIDTH + 3 * CONV_WIDTH
CHUNK = 128
ROPE_BASE = 10000.0
N_GROUPS = 8
EXPERTS_PER_GROUP = 8
N_EXPERTS = 64
TOP_K = 2
EXPERT_FF = 512
MOE_BLOCK = 128
EPS = 1e-6

LANES = 128
SUBLANES = 8
PACKED = D_MODEL // 2
ROW_PIECES = PACKED // LANES


def _quad(rows):
    return rows * ROW_PIECES
EXPERT_LANE0 = N_GROUPS
VMEM_LIMIT = 48 * 1024 * 1024

MIX_TILE = 256
ROW_TILE = 1024
PROJ_PIECE = 512
PIECES_IN_RETENTION = 5
PIECES_AFTER_RETENTION = 0
CHUNK_BLOCKS = 4
CHUNK_ROWS = CHUNK_BLOCKS * MOE_BLOCK
WEIGHT_SLOTS = 3


def _pack_bf16_pair(lo, hi):
    return pltpu.pack_elementwise([lo, hi], packed_dtype=BF16)


def _unpack_bf16_pair(packed):
    lo = pltpu.unpack_elementwise(packed, index=0, packed_dtype=BF16, unpacked_dtype=F32)
    hi = pltpu.unpack_elementwise(packed, index=1, packed_dtype=BF16, unpacked_dtype=F32)
    return lo, hi


def _store_rows(dst_ref, lead, rows, packed):
    for j in range(ROW_PIECES):
        dst_ref[lead + (pl.ds(j, rows, stride=ROW_PIECES), slice(None))] = packed[:, j * LANES:(j + 1) * LANES]


def _load_rows(src_ref, lead, rows):
    los, his = [], []
    for j in range(ROW_PIECES):
        lo, hi = _unpack_bf16_pair(src_ref[lead + (pl.ds(j, rows, stride=ROW_PIECES), slice(None))])
        los.append(lo)
        his.append(hi)
    return los, his


def _row_slab(ref, row):
    return ref.at[pl.ds(pl.multiple_of(row * ROW_PIECES, ROW_PIECES), ROW_PIECES)]


def _zero_rows(zbuf):
    zero = jnp.zeros(zbuf.shape, F32)
    zbuf[...] = _pack_bf16_pair(zero, zero)


def _rms(x, w):
    ms = jnp.mean(x * x, axis=-1, keepdims=True)
    return (x * lax.rsqrt(ms + EPS)) * w


def _in_projection(x_ref, n1_ref, win_ref, proj_ref):
    h = _rms(x_ref[...], n1_ref[...])
    proj_ref[...] = jnp.dot(h.astype(BF16), win_ref[...], preferred_element_type=F32)


def _mixer_router_body(x_ref, xn_ref, n1_ref, win_ref, cos_ref, sin_ref, dmat_ref, kdec_ref, qdec_ref,
                       gnw_ref, convw_ref, wo_ref, n2_ref, wr2_ref, rb_ref,
                       x1_ref, h2p_ref, route_ref, route_t_ref, cnt_ref,
                       state_ref, cu_ref, mix_ref, run_ref, proj_a_ref, proj_b_ref, hn_ref,
                       *, chunk_decay, tiles_per_seq):
    i = pl.program_id(0)
    c = lax.rem(i, tiles_per_seq)

    @pl.when(c == 0)
    def _():
        state_ref[...] = jnp.zeros_like(state_ref)
        cu_ref[0:SUBLANES, :] = jnp.zeros((SUBLANES, CONV_WIDTH), F32)

    @pl.when(i == 0)
    def _():
        run_ref[...] = jnp.zeros_like(run_ref)
        _in_projection(x_ref, n1_ref, win_ref, proj_a_ref)

    def step(proj_ref, proj_next_ref):
        hn_ref[...] = _rms(xn_ref[...], n1_ref[...]).astype(BF16)

        def project_piece(p):
            cols = slice(p * PROJ_PIECE, (p + 1) * PROJ_PIECE)
            proj_next_ref[:, cols] = jnp.dot(hn_ref[...], win_ref[:, cols], preferred_element_type=F32)

        _mix_and_route(proj_ref, x_ref, cos_ref, sin_ref, dmat_ref, kdec_ref, qdec_ref, gnw_ref,
                       convw_ref, wo_ref, n2_ref, wr2_ref, rb_ref,
                       x1_ref, h2p_ref, route_ref, route_t_ref, cnt_ref,
                       state_ref, cu_ref, mix_ref, run_ref, chunk_decay, project_piece)

    @pl.when(lax.rem(i, 2) == 0)
    def _():
        step(proj_a_ref, proj_b_ref)

    @pl.when(lax.rem(i, 2) == 1)
    def _():
        step(proj_b_ref, proj_a_ref)


def _mix_and_route(proj, x_ref, cos_ref, sin_ref, dmat_ref, kdec_ref, qdec_ref, gnw_ref,
                   convw_ref, wo_ref, n2_ref, wr2_ref, rb_ref,
                   x1_ref, h2p_ref, route_ref, route_t_ref, cnt_ref,
                   state_ref, cu_ref, mix_ref, run_ref, chunk_decay, project_piece):
    tm = x_ref.shape[0]
    x = x_ref[...]
    R = RET_WIDTH
    cosf = cos_ref[...]
    sins = sin_ref[...]
    n_pieces = IN_COLS // PROJ_PIECE
    piece = 0

    gb = proj[:, 4 * R:4 * R + CONV_WIDTH]
    gc = proj[:, 4 * R + CONV_WIDTH:4 * R + 2 * CONV_WIDTH]
    u = proj[:, 4 * R + 2 * CONV_WIDTH:4 * R + 3 * CONV_WIDTH]
    cu_ref[SUBLANES:SUBLANES + tm, :] = gc * u
    conv = (convw_ref[2:3, :] * cu_ref[SUBLANES:SUBLANES + tm, :]
            + convw_ref[1:2, :] * cu_ref[SUBLANES - 1:SUBLANES - 1 + tm, :]
            + convw_ref[0:1, :] * cu_ref[SUBLANES - 2:SUBLANES - 2 + tm, :])
    cu_ref[0:SUBLANES, :] = cu_ref[tm:tm + SUBLANES, :]
    x1 = x + jnp.dot((gb * conv).astype(BF16), wo_ref[R:R + CONV_WIDTH, :], preferred_element_type=F32)

    heads = range(RET_HEADS)

    def next_piece():
        nonlocal piece
        if piece < PIECES_IN_RETENTION:
            project_piece(piece)
            piece += 1

    for j in range(tm // CHUNK):
        rows = slice(j * CHUNK, (j + 1) * CHUNK)
        cols = [slice(hh * HEAD_DIM, (hh + 1) * HEAD_DIM) for hh in heads]
        cs = cosf[rows, :]
        sn = sins[rows, :]
        qr, kr, vb = [], [], []
        for hh in heads:
            c0 = hh * HEAD_DIM
            qh = proj[rows, c0:c0 + HEAD_DIM]
            kh = proj[rows, R + c0:R + c0 + HEAD_DIM]
            qr.append(qh * cs + pltpu.roll(qh, HEAD_DIM // 2, 1) * sn)
            kr.append(kh * cs + pltpu.roll(kh, HEAD_DIM // 2, 1) * sn)
            vb.append(proj[rows, 2 * R + c0:2 * R + c0 + HEAD_DIM].astype(BF16))
        next_piece()
        scores = [lax.dot_general(qr[hh].astype(BF16), kr[hh].astype(BF16), (((1,), (1,)), ((), ())),
                                  preferred_element_type=F32) for hh in heads]
        s_prev = [state_ref[hh] for hh in heads]
        inter = [jnp.dot((qr[hh] * qdec_ref[:, cols[hh]]).astype(BF16), s_prev[hh].astype(BF16),
                         preferred_element_type=F32) for hh in heads]
        next_piece()
        kv = [lax.dot_general((kr[hh] * kdec_ref[:, cols[hh]]).astype(BF16), vb[hh],
                              (((0,), (0,)), ((), ())), preferred_element_type=F32) for hh in heads]
        for hh in heads:
            state_ref[hh] = chunk_decay[hh] * s_prev[hh] + kv[hh]
        o = [jnp.dot((scores[hh] * dmat_ref[hh]).astype(BF16), vb[hh], preferred_element_type=F32)
             + inter[hh] for hh in heads]
        next_piece()
        mu = [jnp.mean(o[hh], axis=-1, keepdims=True) for hh in heads]
        d = [o[hh] - mu[hh] for hh in heads]
        var = [jnp.mean(d[hh] * d[hh], axis=-1, keepdims=True) for hh in heads]
        for hh in heads:
            gh = proj[rows, 3 * R + hh * HEAD_DIM:3 * R + (hh + 1) * HEAD_DIM]
            y = d[hh] * lax.rsqrt(var[hh] + EPS) * gnw_ref[:, cols[hh]]
            gate = gh * (1.0 / (1.0 + jnp.exp(-gh)))
            mix_ref[rows, cols[hh]] = (gate * y).astype(BF16)

    while piece < PIECES_IN_RETENTION + PIECES_AFTER_RETENTION:
        project_piece(piece)
        piece += 1

    x1 = x1 + jnp.dot(mix_ref[...], wo_ref[0:R, :], preferred_element_type=F32)
    x1_ref[...] = x1
    h2 = _rms(x1, n2_ref[...])
    _store_rows(h2p_ref, (), tm, _pack_bf16_pair(h2[:, :PACKED], h2[:, PACKED:]))

    hi = h2.astype(BF16)
    lo = (h2 - hi.astype(F32)).astype(BF16)
    wrh = wr2_ref[:, 0:LANES]
    logits = (jnp.dot(hi, wrh, preferred_element_type=F32)
              + jnp.dot(lo, wrh, preferred_element_type=F32)
              + jnp.dot(hi, wr2_ref[:, LANES:2 * LANES], preferred_element_type=F32)
              + rb_ref[...])

    while piece < n_pieces:
        project_piece(piece)
        piece += 1

    lane = lax.broadcasted_iota(I32, (tm, LANES), 1)
    lane_f = lane.astype(F32)
    neg_inf = jnp.float32(-jnp.inf)

    def first_max(v):
        m = jnp.max(v, axis=-1, keepdims=True)
        return m, jnp.min(jnp.where(v == m, lane_f, float(LANES)), axis=-1, keepdims=True)

    gl = jnp.where(lane < N_GROUPS, logits, neg_inf)
    gmax, g_sel = first_max(gl)
    p_group = 1.0 / jnp.sum(jnp.exp(gl - gmax), axis=-1, keepdims=True)

    e_lo = EXPERT_LANE0 + g_sel * EXPERTS_PER_GROUP
    emask = (lane_f >= e_lo) & (lane_f < e_lo + EXPERTS_PER_GROUP)
    el = jnp.where(emask, logits, neg_inf)
    l1, i1 = first_max(el)
    l2, i2 = first_max(jnp.where(lane_f == i1, neg_inf, el))
    r = jnp.exp(l2 - l1)
    gate1 = p_group / (1.0 + r)
    gate2 = gate1 * r

    oh1 = jnp.where(lane_f == i1, 1.0, 0.0)
    oh2 = jnp.where(lane_f == i2, 1.0, 0.0)
    row = lax.broadcasted_iota(I32, (tm, tm), 0)
    col = lax.broadcasted_iota(I32, (tm, tm), 1)
    lower = jnp.where(row > col, 1.0, 0.0).astype(BF16)
    pre1 = jnp.dot(lower, oh1.astype(BF16), preferred_element_type=F32)
    pre2 = jnp.dot(lower, oh2.astype(BF16), preferred_element_type=F32)
    cnt1 = jnp.sum(oh1, axis=0, keepdims=True)
    cnt2 = jnp.sum(oh2, axis=0, keepdims=True)
    run = run_ref[...]
    rank1 = jnp.sum(oh1 * (pre1 + run), axis=-1, keepdims=True)
    rank2 = jnp.sum(oh2 * (pre2 + run + cnt1), axis=-1, keepdims=True)
    run_new = run + cnt1 + cnt2
    run_ref[...] = run_new
    cnt_ref[...] = run_new

    route = jnp.where(lane == 0, i1 - EXPERT_LANE0,
            jnp.where(lane == 1, i2 - EXPERT_LANE0,
            jnp.where(lane == 2, gate1,
            jnp.where(lane == 3, gate2,
            jnp.where(lane == 4, rank1,
            jnp.where(lane == 5, rank2, 0.0))))))
    route_ref[...] = route[:, :SUBLANES]
    route_t_ref[...] = jnp.transpose(route)[:SUBLANES, :]


def _mixer_router(x, n1, win_bf, cosf, sins, dmat, kdec, qdec, gnw, convw, wo_bf, n2, wr2, rb,
                  chunk_decay):
    B, S, D = x.shape
    tm = MIX_TILE
    n = B * S
    tiles_per_seq = S // tm
    n_tiles = n // tm
    x = x.reshape(n, D)
    const2 = lambda i: (0, 0)
    tile = lambda i: (i, 0)
    in_specs = [
        pl.BlockSpec((tm, D), tile),
        pl.BlockSpec((tm, D), lambda i: (jnp.minimum(i + 1, n_tiles - 1), 0)),
        pl.BlockSpec((1, D), const2),
        pl.BlockSpec((D, IN_COLS), const2),
        pl.BlockSpec((tm, HEAD_DIM), lambda i: (lax.rem(i, tiles_per_seq), 0)),
        pl.BlockSpec((tm, HEAD_DIM), lambda i: (lax.rem(i, tiles_per_seq), 0)),
        pl.BlockSpec((RET_HEADS, CHUNK, CHUNK), lambda i: (0, 0, 0)),
        pl.BlockSpec((CHUNK, RET_WIDTH), const2),
        pl.BlockSpec((CHUNK, RET_WIDTH), const2),
        pl.BlockSpec((1, RET_WIDTH), const2),
        pl.BlockSpec((CONV_K, CONV_WIDTH), const2),
        pl.BlockSpec((D, D), const2),
        pl.BlockSpec((1, D), const2),
        pl.BlockSpec((D, 2 * LANES), const2),
        pl.BlockSpec((1, LANES), const2),
    ]
    out_specs = [
        pl.BlockSpec((tm, D), tile),
        pl.BlockSpec((_quad(tm), LANES), tile),
        pl.BlockSpec((tm, SUBLANES), tile),
        pl.BlockSpec((SUBLANES, tm), lambda i: (0, i)),
        pl.BlockSpec((1, LANES), const2),
    ]
    out_shape = [
        jax.ShapeDtypeStruct((n, D), F32),
        jax.ShapeDtypeStruct((_quad(n), LANES), U32),
        jax.ShapeDtypeStruct((n, SUBLANES), F32),
        jax.ShapeDtypeStruct((SUBLANES, n), F32),
        jax.ShapeDtypeStruct((1, LANES), F32),
    ]
    scratch = [
        pltpu.VMEM((RET_HEADS, HEAD_DIM, HEAD_DIM), F32),
        pltpu.VMEM((tm + SUBLANES, CONV_WIDTH), F32),
        pltpu.VMEM((tm, RET_WIDTH), BF16),
        pltpu.VMEM((1, LANES), F32),
        pltpu.VMEM((tm, IN_COLS), F32),
        pltpu.VMEM((tm, IN_COLS), F32),
        pltpu.VMEM((tm, D), BF16),
    ]
    return pl.pallas_call(
        functools.partial(_mixer_router_body, chunk_decay=chunk_decay, tiles_per_seq=tiles_per_seq),
        grid=(n_tiles,), in_specs=in_specs, out_specs=out_specs, out_shape=out_shape,
        scratch_shapes=scratch,
        compiler_params=pltpu.CompilerParams(
            dimension_semantics=("arbitrary",), vmem_limit_bytes=VMEM_LIMIT),
        name="mixer_router",
    )(x, x, n1, win_bf, cosf, sins, dmat, kdec, qdec, gnw, convw, wo_bf, n2, wr2, rb)


def _row_copy(src_ref, src_row, dst_ref, dst_row, sem):
    return pltpu.make_async_copy(_row_slab(src_ref, src_row), _row_slab(dst_ref, dst_row), sem)


def _zero_block_copy(zbuf, dst_ref, blk, sem):
    r0 = pl.multiple_of(blk * _quad(MOE_BLOCK), _quad(MOE_BLOCK))
    return pltpu.make_async_copy(zbuf, dst_ref.at[pl.ds(r0, _quad(MOE_BLOCK))], sem)


def _dispatch_body(dest0_ref, dest1_ref, fill_ref, h2p_ref, xs_ref, zbuf, sem, z_sem):
    tt = dest0_ref.shape[0]

    @pl.when(pl.program_id(0) == 0)
    def _():
        _zero_rows(zbuf)
        used = fill_ref[N_EXPERTS]
        total = xs_ref.shape[0] // _quad(MOE_BLOCK)

        def expert_pad(e, carry):
            blk = fill_ref[e]

            @pl.when(blk >= 0)
            def _():
                _zero_block_copy(zbuf, xs_ref, blk, z_sem).start()
            return carry

        def expert_pad_wait(e, carry):
            @pl.when(fill_ref[e] >= 0)
            def _():
                _zero_block_copy(zbuf, xs_ref, 0, z_sem).wait()
            return carry

        def tail(blk, carry):
            _zero_block_copy(zbuf, xs_ref, blk, z_sem).start()
            return carry

        def tail_wait(blk, carry):
            _zero_block_copy(zbuf, xs_ref, 0, z_sem).wait()
            return carry

        lax.fori_loop(0, N_EXPERTS, expert_pad, 0)
        lax.fori_loop(used, total, tail, 0)
        lax.fori_loop(0, N_EXPERTS, expert_pad_wait, 0)
        lax.fori_loop(used, total, tail_wait, 0)

    def issue(t, carry):
        _row_copy(h2p_ref, t, xs_ref, dest0_ref[t], sem).start(priority=0)
        _row_copy(h2p_ref, t, xs_ref, dest1_ref[t], sem).start(priority=1)
        return carry

    lax.fori_loop(0, tt, issue, 0, unroll=8)

    def drain(t, carry):
        _row_copy(h2p_ref, 0, xs_ref, 0, sem).wait()
        _row_copy(h2p_ref, 0, xs_ref, 0, sem).wait()
        return carry

    lax.fori_loop(0, tt, drain, 0, unroll=8)


def _dispatch(dest0, dest1, fill, h2p, xs_rows):
    n = dest0.shape[0]
    assert h2p.shape == (_quad(n), LANES)
    tt = ROW_TILE
    return pl.pallas_call(
        _dispatch_body,
        grid=(n // tt,),
        in_specs=[
            pl.BlockSpec((tt,), lambda i: (i,), memory_space=pltpu.SMEM),
            pl.BlockSpec((tt,), lambda i: (i,), memory_space=pltpu.SMEM),
            pl.BlockSpec(memory_space=pltpu.SMEM),
            pl.BlockSpec((_quad(tt), LANES), lambda i: (i, 0)),
        ],
        out_specs=pl.BlockSpec(memory_space=pl.ANY),
        out_shape=jax.ShapeDtypeStruct((_quad(xs_rows), LANES), U32),
        scratch_shapes=[
            pltpu.VMEM((_quad(MOE_BLOCK), LANES), U32),
            pltpu.SemaphoreType.DMA,
            pltpu.SemaphoreType.DMA,
        ],
        compiler_params=pltpu.CompilerParams(dimension_semantics=("arbitrary",)),
        name="dispatch",
    )(dest0, dest1, fill, h2p)


def _experts_body(ord_ref, row0_ref, nblk_ref, oe_ref, meta_ref, xs_ref, wg_ref, wu_ref, wd_ref, ys_ref,
                  xbuf, obuf, zbuf, wg_ring, wu_ring, wd_ring, wgb_ref, wub_ref, wdb_ref,
                  in_sem, out_sem, z_sem, w_sem):
    i = pl.program_id(0)
    n_chunks = meta_ref[0]
    n_ord = meta_ref[2]
    slot = lax.rem(i, 2)

    def weight_copies(k, ws):
        e = oe_ref[k]
        return (pltpu.make_async_copy(wg_ref.at[e], wg_ring.at[ws], w_sem.at[ws]),
                pltpu.make_async_copy(wu_ref.at[e], wu_ring.at[ws], w_sem.at[ws]),
                pltpu.make_async_copy(wd_ref.at[e], wd_ring.at[ws], w_sem.at[ws]))

    def fetch_weights(k):
        for c in weight_copies(k, lax.rem(k, WEIGHT_SLOTS)):
            c.start()

    def in_copy(chunk, s, piece):
        r0 = pl.multiple_of(_quad(row0_ref[chunk] + piece * MOE_BLOCK), _quad(MOE_BLOCK))
        return pltpu.make_async_copy(xs_ref.at[pl.ds(r0, _quad(MOE_BLOCK))],
                                     xbuf.at[s, pl.ds(piece * _quad(MOE_BLOCK), _quad(MOE_BLOCK))], in_sem.at[s])

    def out_copy(chunk, s, piece):
        r0 = pl.multiple_of(_quad(row0_ref[chunk] + piece * MOE_BLOCK), _quad(MOE_BLOCK))
        return pltpu.make_async_copy(obuf.at[s, pl.ds(piece * _quad(MOE_BLOCK), _quad(MOE_BLOCK))],
                                     ys_ref.at[pl.ds(r0, _quad(MOE_BLOCK))], out_sem.at[s])

    def for_pieces(chunk, fn):
        for piece in range(CHUNK_BLOCKS):
            @pl.when(piece < nblk_ref[chunk])
            def _():
                fn(piece)

    @pl.when(i == 0)
    def _():
        for_pieces(0, lambda piece: in_copy(0, 0, piece).start())
        for k in range(WEIGHT_SLOTS - 1):
            @pl.when(k < n_ord)
            def _():
                fetch_weights(k)

    @pl.when(i + 1 < n_chunks)
    def _():
        for_pieces(i + 1, lambda piece: in_copy(i + 1, 1 - slot, piece).start())

    @pl.when(i < n_chunks)
    def _():
        k = ord_ref[i]

        @pl.when((i == 0) | (k != ord_ref[jnp.maximum(i - 1, 0)]))
        def _():
            @pl.when(k + WEIGHT_SLOTS - 1 < n_ord)
            def _():
                fetch_weights(k + WEIGHT_SLOTS - 1)

            ws = lax.rem(k, WEIGHT_SLOTS)
            for c in weight_copies(k, ws):
                c.wait()
            wgb_ref[...] = wg_ring[ws].astype(BF16)
            wub_ref[...] = wu_ring[ws].astype(BF16)
            wdb_ref[...] = wd_ring[ws].astype(BF16)

        for_pieces(i, lambda piece: in_copy(i, slot, piece).wait())

        @pl.when(i >= 2)
        def _():
            for_pieces(i - 2, lambda piece: out_copy(i - 2, slot, piece).wait())

        for m in range(1, CHUNK_BLOCKS + 1):
            @pl.when(nblk_ref[i] == m)
            def _():
                rows = m * MOE_BLOCK
                los, his = _load_rows(xbuf, (slot,), rows)
                xb = jnp.concatenate([v.astype(BF16) for v in los + his], axis=-1)
                g = jnp.dot(xb, wgb_ref[...], preferred_element_type=F32)
                u = jnp.dot(xb, wub_ref[...], preferred_element_type=F32)
                hmid = (g * (1.0 / (1.0 + jnp.exp(-g)))) * u
                y = jnp.dot(hmid.astype(BF16), wdb_ref[...], preferred_element_type=F32)
                packed = _pack_bf16_pair(y[:, :PACKED], y[:, PACKED:])
                _store_rows(obuf, (slot,), rows, packed)

        for_pieces(i, lambda piece: out_copy(i, slot, piece).start())

    @pl.when(i == n_chunks - 1)
    def _():
        @pl.when(i >= 1)
        def _():
            for_pieces(i - 1, lambda piece: out_copy(i - 1, 1 - slot, piece).wait())
        for_pieces(i, lambda piece: out_copy(i, slot, piece).wait())

        _zero_rows(zbuf)
        used = meta_ref[1]
        total = ys_ref.shape[0] // _quad(MOE_BLOCK)

        def z_start(blk, carry):
            _zero_block_copy(zbuf, ys_ref, blk, z_sem).start()
            return carry

        def z_wait(blk, carry):
            _zero_block_copy(zbuf, ys_ref, blk, z_sem).wait()
            return carry

        lax.fori_loop(used, total, z_start, 0)
        lax.fori_loop(used, total, z_wait, 0)


def _experts(chunk_ord, chunk_row0, chunk_nblk, ord_expert, meta, xs, n_rows_out, w_gate, w_up, w_down):
    any_space = pl.BlockSpec(memory_space=pl.ANY)
    return pl.pallas_call(
        _experts_body,
        grid_spec=pltpu.PrefetchScalarGridSpec(
            num_scalar_prefetch=5,
            grid=(chunk_ord.shape[0],),
            in_specs=[any_space, any_space, any_space, any_space],
            out_specs=any_space,
            scratch_shapes=[
                pltpu.VMEM((2, _quad(CHUNK_ROWS), LANES), U32),
                pltpu.VMEM((2, _quad(CHUNK_ROWS), LANES), U32),
                pltpu.VMEM((_quad(MOE_BLOCK), LANES), U32),
                pltpu.VMEM((WEIGHT_SLOTS, D_MODEL, EXPERT_FF), F32),
                pltpu.VMEM((WEIGHT_SLOTS, D_MODEL, EXPERT_FF), F32),
                pltpu.VMEM((WEIGHT_SLOTS, EXPERT_FF, D_MODEL), F32),
                pltpu.VMEM((D_MODEL, EXPERT_FF), BF16),
                pltpu.VMEM((D_MODEL, EXPERT_FF), BF16),
                pltpu.VMEM((EXPERT_FF, D_MODEL), BF16),
                pltpu.SemaphoreType.DMA((2,)),
                pltpu.SemaphoreType.DMA((2,)),
                pltpu.SemaphoreType.DMA,
                pltpu.SemaphoreType.DMA((WEIGHT_SLOTS,)),
            ],
        ),
        out_shape=jax.ShapeDtypeStruct((_quad(n_rows_out), LANES), U32),
        compiler_params=pltpu.CompilerParams(
            dimension_semantics=("arbitrary",), vmem_limit_bytes=VMEM_LIMIT),
        name="experts",
    )(chunk_ord, chunk_row0, chunk_nblk, ord_expert, meta, xs, w_gate, w_up, w_down)


def _combine_body(dest0_ref, dest1_ref, next0_ref, next1_ref, x1_ref, route_ref, fw_ref, ys_ref, out_ref,
                  y0_ref, y1_ref, sem):
    i = pl.program_id(0)
    tt = x1_ref.shape[0]
    slot = lax.rem(i, 2)

    def gather(d0_ref, d1_ref, s):
        def issue(t, carry):
            _row_copy(ys_ref, d0_ref[t], y0_ref.at[s], t, sem.at[s]).start(priority=0)
            _row_copy(ys_ref, d1_ref[t], y1_ref.at[s], t, sem.at[s]).start(priority=1)
            return carry

        lax.fori_loop(0, tt, issue, 0, unroll=8)

    @pl.when(i == 0)
    def _():
        gather(dest0_ref, dest1_ref, 0)

    @pl.when(i + 1 < pl.num_programs(0))
    def _():
        gather(next0_ref, next1_ref, 1 - slot)

    def drain(t, carry):
        _row_copy(ys_ref, 0, y0_ref.at[slot], 0, sem.at[slot]).wait()
        _row_copy(ys_ref, 0, y1_ref.at[slot], 0, sem.at[slot]).wait()
        return carry

    lax.fori_loop(0, tt, drain, 0, unroll=8)

    route = route_ref[...]
    g0 = route[:, 2:3]
    g1 = route[:, 3:4]
    a_lo, a_hi = (jnp.concatenate(v, axis=-1) for v in _load_rows(y0_ref, (slot,), tt))
    b_lo, b_hi = (jnp.concatenate(v, axis=-1) for v in _load_rows(y1_ref, (slot,), tt))
    x1 = x1_ref[...]
    z_lo = x1[:, :PACKED] + (a_lo * g0 + b_lo * g1)
    z_hi = x1[:, PACKED:] + (a_hi * g0 + b_hi * g1)
    ms = (jnp.sum(z_lo * z_lo, axis=-1, keepdims=True)
          + jnp.sum(z_hi * z_hi, axis=-1, keepdims=True)) * (1.0 / D_MODEL)
    inv = lax.rsqrt(ms + EPS)
    out_ref[:, :PACKED] = (z_lo * inv) * fw_ref[:, :PACKED]
    out_ref[:, PACKED:] = (z_hi * inv) * fw_ref[:, PACKED:]


def _combine(dest0, dest1, x1, route, fw, ys):
    n, d = x1.shape
    tt = ROW_TILE
    steps = n // tt
    this_tile = lambda i: (i,)
    next_tile = lambda i: (jnp.minimum(i + 1, steps - 1),)
    return pl.pallas_call(
        _combine_body,
        grid=(steps,),
        in_specs=[
            pl.BlockSpec((tt,), this_tile, memory_space=pltpu.SMEM),
            pl.BlockSpec((tt,), this_tile, memory_space=pltpu.SMEM),
            pl.BlockSpec((tt,), next_tile, memory_space=pltpu.SMEM),
            pl.BlockSpec((tt,), next_tile, memory_space=pltpu.SMEM),
            pl.BlockSpec((tt, d), lambda i: (i, 0)),
            pl.BlockSpec((tt, SUBLANES), lambda i: (i, 0)),
            pl.BlockSpec((1, d), lambda i: (0, 0)),
            pl.BlockSpec(memory_space=pl.ANY),
        ],
        out_specs=pl.BlockSpec((tt, d), lambda i: (i, 0)),
        out_shape=jax.ShapeDtypeStruct((n, d), F32),
        scratch_shapes=[
            pltpu.VMEM((2, _quad(tt), LANES), U32),
            pltpu.VMEM((2, _quad(tt), LANES), U32),
            pltpu.SemaphoreType.DMA((2,)),
        ],
        compiler_params=pltpu.CompilerParams(
            dimension_semantics=("arbitrary",), vmem_limit_bytes=VMEM_LIMIT),
        name="combine",
    )(dest0, dest1, dest0, dest1, x1, route, fw, ys)


def _retention_tables(seq):
    half = HEAD_DIM // 2
    inv = ROPE_BASE ** (-jnp.arange(half, dtype=F32) / half)
    ang = jnp.arange(seq).astype(F32)[:, None] * inv[None, :]
    cos, sin = jnp.cos(ang), jnp.sin(ang)
    cosf = jnp.concatenate([cos, cos], axis=-1)
    sins = jnp.concatenate([-sin, sin], axis=-1)
    log_g = jnp.log1p(-(2.0 ** (-5.0 - jnp.arange(RET_HEADS, dtype=F32))))
    idx = jnp.arange(CHUNK, dtype=F32)
    rel = idx[:, None] - idx[None, :]
    causal = rel >= 0
    scale = HEAD_DIM ** -0.5
    dmat = jnp.where(causal[None], jnp.exp(log_g[:, None, None] * jnp.where(causal, rel, 0.0)[None]), 0.0)
    k_decay = jnp.exp(log_g[:, None] * (CHUNK - 1 - idx)[None, :])
    q_decay = jnp.exp(log_g[:, None] * (idx + 1)[None, :])
    kdec = jnp.repeat(k_decay.T, HEAD_DIM, axis=1)
    qdec = jnp.repeat(q_decay.T, HEAD_DIM, axis=1) * scale
    return cosf, sins, dmat * scale, kdec, qdec


def _chunk_decay():
    log_g = np.log1p(-(2.0 ** (-5.0 - np.arange(RET_HEADS, dtype=np.float64))))
    return tuple(float(v) for v in np.exp(log_g * CHUNK))


def _layer(x, norm1_w, w_in, ret_gn_w, conv_w, w_o, norm2_w, router_g_w, router_g_b,
           router_e_w, router_e_b, w_gate, w_up, w_down, final_w):
    B, S, D = x.shape
    n = B * S
    cosf, sins, dmat, kdec, qdec = _retention_tables(S)

    spare = LANES - N_GROUPS - N_EXPERTS
    wr = jnp.concatenate([router_g_w, jnp.transpose(router_e_w, (1, 0, 2)).reshape(D, N_EXPERTS),
                          jnp.zeros((D, spare), F32)], axis=1)
    wrh = wr.astype(BF16)
    wrl = (wr - wrh.astype(F32)).astype(BF16)
    wr2 = jnp.concatenate([wrh, wrl], axis=1)
    rb = jnp.concatenate([router_g_b, router_e_b.reshape(-1), jnp.zeros((spare,), F32)])[None]

    x1, h2p, route, route_t, counts = _mixer_router(
        x, norm1_w[None], w_in.astype(BF16), cosf, sins, dmat, kdec, qdec, ret_gn_w[None], conv_w,
        w_o.astype(BF16), norm2_w[None], wr2, rb, _chunk_decay())

    expert = route_t[0:2].astype(I32)
    rank = route_t[4:6].astype(I32)
    cnt = counts[0, EXPERT_LANE0:EXPERT_LANE0 + N_EXPERTS].astype(I32)
    padded = ((cnt + MOE_BLOCK - 1) // MOE_BLOCK) * MOE_BLOCK
    pad_ends = jnp.cumsum(padded)
    pad_starts = pad_ends - padded
    eids = jnp.arange(N_EXPERTS, dtype=I32)
    start_of = jnp.sum(jnp.where(expert[:, :, None] == eids, pad_starts, 0), axis=-1)
    dest = start_of + rank
    n_blocks = -(-(n * TOP_K) // MOE_BLOCK) + N_EXPERTS
    used_blocks = pad_ends[-1] // MOE_BLOCK

    n_chunk_slots = -(-(n * TOP_K) // CHUNK_ROWS) + N_EXPERTS
    chunks_of = (padded + CHUNK_ROWS - 1) // CHUNK_ROWS
    chunk_ends = jnp.cumsum(chunks_of)
    n_chunks = chunk_ends[-1]
    ci = jnp.minimum(jnp.arange(n_chunk_slots, dtype=I32), n_chunks - 1)
    chunk_e = jnp.minimum(jnp.sum((chunk_ends[None, :] <= ci[:, None]).astype(I32), axis=1), N_EXPERTS - 1)
    onehot_e = chunk_e[:, None] == eids
    pick = lambda v: jnp.sum(jnp.where(onehot_e, v, 0), axis=-1)
    k = ci - pick(chunk_ends - chunks_of)
    blocks_e = pick(padded) // MOE_BLOCK
    first_nblk = blocks_e - (pick(chunks_of) - 1) * CHUNK_BLOCKS
    chunk_nblk = jnp.where(k == 0, first_nblk, CHUNK_BLOCKS)
    chunk_row0 = pick(pad_starts) + jnp.where(k == 0, 0, first_nblk + (k - 1) * CHUNK_BLOCKS) * MOE_BLOCK
    has_rows = padded > 0
    ord_of = jnp.cumsum(has_rows.astype(I32)) - 1
    n_ord = jnp.sum(has_rows.astype(I32))
    chunk_ord = pick(ord_of)
    ord_expert = jnp.sum(jnp.where(has_rows[None, :] & (ord_of[None, :] == eids[:, None]), eids[None, :], 0), axis=1)
    meta = jnp.stack([n_chunks, used_blocks, n_ord]).astype(I32)

    xs_rows = n_blocks * MOE_BLOCK
    last_block = jnp.where(padded > 0, pad_ends // MOE_BLOCK - 1, -1)
    fill = jnp.concatenate([last_block, used_blocks[None]]).astype(I32)
    xs = _dispatch(dest[0], dest[1], fill, h2p, xs_rows)
    ys = _experts(chunk_ord.astype(I32), chunk_row0.astype(I32), chunk_nblk.astype(I32),
                  ord_expert.astype(I32), meta, xs, n_blocks * MOE_BLOCK, w_gate, w_up, w_down)
    out = _combine(dest[0], dest[1], x1, route, final_w[None], ys)
    return out.reshape(B, S, D)


def kernel(x, norm1_w, w_in, ret_gn_w, conv_w, w_o, norm2_w, router_g_w, router_g_b, router_e_w,
           router_e_b, w_gate, w_up, w_down, final_norm_w):
    depth = norm1_w.shape[0]
    assert depth == 1, "the final RMSNorm is fused into the last layer's combine step"
    return _layer(x, norm1_w[0], w_in[0], ret_gn_w[0], conv_w[0], w_o[0], norm2_w[0],
                  router_g_w[0], router_g_b[0], router_e_w[0], router_e_b[0],
                  w_gate[0], w_up[0], w_down[0], final_norm_w)
```

```python
import functools

import jax
import jax.numpy as jnp
import numpy as np
from jax import lax
from jax.experimental import pallas as pl
from jax.experimental.pallas import tpu as pltpu

F32 = jnp.float32
BF16 = jnp.bfloat16
U32 = jnp.uint32
I32 = jnp.int32

D_MODEL = 1024
RET_WIDTH = 512
RET_HEADS = 4
HEAD_DIM = 128
CONV_WIDTH = 512
CONV_K = 3
IN_COLS = 4 * RET_WIDTH + 3 * CONV_WIDTH
CHUNK = 128
ROPE_BASE = 10000.0
N_GROUPS = 8
EXPERTS_PER_GROUP = 8
N_EXPERTS = 64
TOP_K = 2
EXPERT_FF = 512
MOE_BLOCK = 128
EPS = 1e-6

LANES = 128
SUBLANES = 8
PACKED = D_MODEL // 2
ROW_PIECES = PACKED // LANES


def _quad(rows):
    return rows * ROW_PIECES
EXPERT_LANE0 = N_GROUPS
VMEM_LIMIT = 56 * 1024 * 1024

MIX_TILE = 512
ROW_TILE = 1024
PROJ_PIECE = 512
PIECES_IN_RETENTION = 5
PIECES_AFTER_RETENTION = 0
CHUNK_BLOCKS = 4
CHUNK_ROWS = CHUNK_BLOCKS * MOE_BLOCK
WEIGHT_SLOTS = 3


def _pack_bf16_pair(lo, hi):
    return pltpu.pack_elementwise([lo, hi], packed_dtype=BF16)


def _unpack_bf16_pair(packed):
    lo = pltpu.unpack_elementwise(packed, index=0, packed_dtype=BF16, unpacked_dtype=F32)
    hi = pltpu.unpack_elementwise(packed, index=1, packed_dtype=BF16, unpacked_dtype=F32)
    return lo, hi


def _store_rows(dst_ref, lead, rows, packed):
    for j in range(ROW_PIECES):
        dst_ref[lead + (pl.ds(j, rows, stride=ROW_PIECES), slice(None))] = packed[:, j * LANES:(j + 1) * LANES]


def _load_rows(src_ref, lead, rows):
    los, his = [], []
    for j in range(ROW_PIECES):
        lo, hi = _unpack_bf16_pair(src_ref[lead + (pl.ds(j, rows, stride=ROW_PIECES), slice(None))])
        los.append(lo)
        his.append(hi)
    return los, his


def _row_slab(ref, row):
    return ref.at[pl.ds(pl.multiple_of(row * ROW_PIECES, ROW_PIECES), ROW_PIECES)]


def _zero_rows(zbuf):
    zero = jnp.zeros(zbuf.shape, F32)
    zbuf[...] = _pack_bf16_pair(zero, zero)


def _rms(x, w):
    ms = jnp.mean(x * x, axis=-1, keepdims=True)
    return (x * lax.rsqrt(ms + EPS)) * w


def _in_projection(x_ref, n1_ref, win_ref, proj_ref):
    h = _rms(x_ref[...], n1_ref[...])
    proj_ref[...] = jnp.dot(h.astype(BF16), win_ref[...], preferred_element_type=F32)


def _mixer_router_body(x_ref, xn_ref, n1_ref, win_ref, cos_ref, sin_ref, dmat_ref, kdec_ref, qdec_ref,
                       gnw_ref, convw_ref, wo_ref, n2_ref, wr2_ref, rb_ref,
                       x1_ref, h2p_ref, route_ref, route_t_ref, cnt_ref,
                       state_ref, cu_ref, mix_ref, run_ref, proj_a_ref, proj_b_ref, hn_ref,
                       *, chunk_decay, tiles_per_seq):
    i = pl.program_id(0)
    c = lax.rem(i, tiles_per_seq)

    @pl.when(c == 0)
    def _():
        state_ref[...] = jnp.zeros_like(state_ref)
        cu_ref[0:SUBLANES, :] = jnp.zeros((SUBLANES, CONV_WIDTH), F32)

    @pl.when(i == 0)
    def _():
        run_ref[...] = jnp.zeros_like(run_ref)
        _in_projection(x_ref, n1_ref, win_ref, proj_a_ref)

    def step(proj_ref, proj_next_ref):
        hn_ref[...] = _rms(xn_ref[...], n1_ref[...]).astype(BF16)

        def project_piece(p):
            cols = slice(p * PROJ_PIECE, (p + 1) * PROJ_PIECE)
            proj_next_ref[:, cols] = jnp.dot(hn_ref[...], win_ref[:, cols], preferred_element_type=F32)

        _mix_and_route(proj_ref, x_ref, cos_ref, sin_ref, dmat_ref, kdec_ref, qdec_ref, gnw_ref,
                       convw_ref, wo_ref, n2_ref, wr2_ref, rb_ref,
                       x1_ref, h2p_ref, route_ref, route_t_ref, cnt_ref,
                       state_ref, cu_ref, mix_ref, run_ref, chunk_decay, project_piece)

    @pl.when(lax.rem(i, 2) == 0)
    def _():
        step(proj_a_ref, proj_b_ref)

    @pl.when(lax.rem(i, 2) == 1)
    def _():
        step(proj_b_ref, proj_a_ref)


def _mix_and_route(proj, x_ref, cos_ref, sin_ref, dmat_ref, kdec_ref, qdec_ref, gnw_ref,
                   convw_ref, wo_ref, n2_ref, wr2_ref, rb_ref,
                   x1_ref, h2p_ref, route_ref, route_t_ref, cnt_ref,
                   state_ref, cu_ref, mix_ref, run_ref, chunk_decay, project_piece):
    tm = x_ref.shape[0]
    x = x_ref[...]
    R = RET_WIDTH
    cosf = cos_ref[...]
    sins = sin_ref[...]
    n_pieces = IN_COLS // PROJ_PIECE
    piece = 0

    gb = proj[:, 4 * R:4 * R + CONV_WIDTH]
    gc = proj[:, 4 * R + CONV_WIDTH:4 * R + 2 * CONV_WIDTH]
    u = proj[:, 4 * R + 2 * CONV_WIDTH:4 * R + 3 * CONV_WIDTH]
    cu_ref[SUBLANES:SUBLANES + tm, :] = gc * u
    conv = (convw_ref[2:3, :] * cu_ref[SUBLANES:SUBLANES + tm, :]
            + convw_ref[1:2, :] * cu_ref[SUBLANES - 1:SUBLANES - 1 + tm, :]
            + convw_ref[0:1, :] * cu_ref[SUBLANES - 2:SUBLANES - 2 + tm, :])
    cu_ref[0:SUBLANES, :] = cu_ref[tm:tm + SUBLANES, :]
    x1 = x + jnp.dot((gb * conv).astype(BF16), wo_ref[R:R + CONV_WIDTH, :], preferred_element_type=F32)

    heads = range(RET_HEADS)

    def next_piece():
        nonlocal piece
        if piece < PIECES_IN_RETENTION:
            project_piece(piece)
            piece += 1

    for j in range(tm // CHUNK):
        rows = slice(j * CHUNK, (j + 1) * CHUNK)
        cols = [slice(hh * HEAD_DIM, (hh + 1) * HEAD_DIM) for hh in heads]
        cs = cosf[rows, :]
        sn = sins[rows, :]
        qr, kr, vb = [], [], []
        for hh in heads:
            c0 = hh * HEAD_DIM
            qh = proj[rows, c0:c0 + HEAD_DIM]
            kh = proj[rows, R + c0:R + c0 + HEAD_DIM]
            qr.append(qh * cs + pltpu.roll(qh, HEAD_DIM // 2, 1) * sn)
            kr.append(kh * cs + pltpu.roll(kh, HEAD_DIM // 2, 1) * sn)
            vb.append(proj[rows, 2 * R + c0:2 * R + c0 + HEAD_DIM].astype(BF16))
        next_piece()
        scores = [lax.dot_general(qr[hh].astype(BF16), kr[hh].astype(BF16), (((1,), (1,)), ((), ())),
                                  preferred_element_type=F32) for hh in heads]
        s_prev = [state_ref[hh] for hh in heads]
        inter = [jnp.dot((qr[hh] * qdec_ref[:, cols[hh]]).astype(BF16), s_prev[hh].astype(BF16),
                         preferred_element_type=F32) for hh in heads]
        next_piece()
        kv = [lax.dot_general((kr[hh] * kdec_ref[:, cols[hh]]).astype(BF16), vb[hh],
                              (((0,), (0,)), ((), ())), preferred_element_type=F32) for hh in heads]
        for hh in heads:
            state_ref[hh] = chunk_decay[hh] * s_prev[hh] + kv[hh]
        o = [jnp.dot((scores[hh] * dmat_ref[hh]).astype(BF16), vb[hh], preferred_element_type=F32)
             + inter[hh] for hh in heads]
        next_piece()
        mu = [jnp.mean(o[hh], axis=-1, keepdims=True) for hh in heads]
        d = [o[hh] - mu[hh] for hh in heads]
        var = [jnp.mean(d[hh] * d[hh], axis=-1, keepdims=True) for hh in heads]
        for hh in heads:
            gh = proj[rows, 3 * R + hh * HEAD_DIM:3 * R + (hh + 1) * HEAD_DIM]
            y = d[hh] * lax.rsqrt(var[hh] + EPS) * gnw_ref[:, cols[hh]]
            gate = gh * (1.0 / (1.0 + jnp.exp(-gh)))
            mix_ref[rows, cols[hh]] = (gate * y).astype(BF16)

    while piece < PIECES_IN_RETENTION + PIECES_AFTER_RETENTION:
        project_piece(piece)
        piece += 1

    x1 = x1 + jnp.dot(mix_ref[...], wo_ref[0:R, :], preferred_element_type=F32)
    x1_ref[...] = x1
    h2 = _rms(x1, n2_ref[...])
    _store_rows(h2p_ref, (), tm, _pack_bf16_pair(h2[:, :PACKED], h2[:, PACKED:]))

    hi = h2.astype(BF16)
    lo = (h2 - hi.astype(F32)).astype(BF16)
    wrh = wr2_ref[:, 0:LANES]
    logits = (jnp.dot(hi, wrh, preferred_element_type=F32)
              + jnp.dot(lo, wrh, preferred_element_type=F32)
              + jnp.dot(hi, wr2_ref[:, LANES:2 * LANES], preferred_element_type=F32)
              + rb_ref[...])

    while piece < n_pieces:
        project_piece(piece)
        piece += 1

    lane = lax.broadcasted_iota(I32, (tm, LANES), 1)
    lane_f = lane.astype(F32)
    neg_inf = jnp.float32(-jnp.inf)

    def first_max(v):
        m = jnp.max(v, axis=-1, keepdims=True)
        return m, jnp.min(jnp.where(v == m, lane_f, float(LANES)), axis=-1, keepdims=True)

    gl = jnp.where(lane < N_GROUPS, logits, neg_inf)
    gmax, g_sel = first_max(gl)
    p_group = 1.0 / jnp.sum(jnp.exp(gl - gmax), axis=-1, keepdims=True)

    e_lo = EXPERT_LANE0 + g_sel * EXPERTS_PER_GROUP
    emask = (lane_f >= e_lo) & (lane_f < e_lo + EXPERTS_PER_GROUP)
    el = jnp.where(emask, logits, neg_inf)
    l1, i1 = first_max(el)
    l2, i2 = first_max(jnp.where(lane_f == i1, neg_inf, el))
    r = jnp.exp(l2 - l1)
    gate1 = p_group / (1.0 + r)
    gate2 = gate1 * r

    oh1 = jnp.where(lane_f == i1, 1.0, 0.0)
    oh2 = jnp.where(lane_f == i2, 1.0, 0.0)
    row = lax.broadcasted_iota(I32, (tm, tm), 0)
    col = lax.broadcasted_iota(I32, (tm, tm), 1)
    lower = jnp.where(row > col, 1.0, 0.0).astype(BF16)
    pre1 = jnp.dot(lower, oh1.astype(BF16), preferred_element_type=F32)
    pre2 = jnp.dot(lower, oh2.astype(BF16), preferred_element_type=F32)
    cnt1 = jnp.sum(oh1, axis=0, keepdims=True)
    cnt2 = jnp.sum(oh2, axis=0, keepdims=True)
    run = run_ref[...]
    rank1 = jnp.sum(oh1 * (pre1 + run), axis=-1, keepdims=True)
    rank2 = jnp.sum(oh2 * (pre2 + run + cnt1), axis=-1, keepdims=True)
    run_new = run + cnt1 + cnt2
    run_ref[...] = run_new
    cnt_ref[...] = run_new

    route = jnp.where(lane == 0, i1 - EXPERT_LANE0,
            jnp.where(lane == 1, i2 - EXPERT_LANE0,
            jnp.where(lane == 2, gate1,
            jnp.where(lane == 3, gate2,
            jnp.where(lane == 4, rank1,
            jnp.where(lane == 5, rank2, 0.0))))))
    route_ref[...] = route[:, :SUBLANES]
    route_t_ref[...] = jnp.transpose(route)[:SUBLANES, :]


def _mixer_router(x, n1, win_bf, cosf, sins, dmat, kdec, qdec, gnw, convw, wo_bf, n2, wr2, rb,
                  chunk_decay):
    B, S, D = x.shape
    tm = MIX_TILE
    n = B * S
    tiles_per_seq = S // tm
    n_tiles = n // tm
    x = x.reshape(n, D)
    const2 = lambda i: (0, 0)
    tile = lambda i: (i, 0)
    in_specs = [
        pl.BlockSpec((tm, D), tile),
        pl.BlockSpec((tm, D), lambda i: (jnp.minimum(i + 1, n_tiles - 1), 0)),
        pl.BlockSpec((1, D), const2),
        pl.BlockSpec((D, IN_COLS), const2),
        pl.BlockSpec((tm, HEAD_DIM), lambda i: (lax.rem(i, tiles_per_seq), 0)),
        pl.BlockSpec((tm, HEAD_DIM), lambda i: (lax.rem(i, tiles_per_seq), 0)),
        pl.BlockSpec((RET_HEADS, CHUNK, CHUNK), lambda i: (0, 0, 0)),
        pl.BlockSpec((CHUNK, RET_WIDTH), const2),
        pl.BlockSpec((CHUNK, RET_WIDTH), const2),
        pl.BlockSpec((1, RET_WIDTH), const2),
        pl.BlockSpec((CONV_K, CONV_WIDTH), const2),
        pl.BlockSpec((D, D), const2),
        pl.BlockSpec((1, D), const2),
        pl.BlockSpec((D, 2 * LANES), const2),
        pl.BlockSpec((1, LANES), const2),
    ]
    out_specs = [
        pl.BlockSpec((tm, D), tile),
        pl.BlockSpec((_quad(tm), LANES), tile),
        pl.BlockSpec((tm, SUBLANES), tile),
        pl.BlockSpec((SUBLANES, tm), lambda i: (0, i)),
        pl.BlockSpec((1, LANES), const2),
    ]
    out_shape = [
        jax.ShapeDtypeStruct((n, D), F32),
        jax.ShapeDtypeStruct((_quad(n), LANES), U32),
        jax.ShapeDtypeStruct((n, SUBLANES), F32),
        jax.ShapeDtypeStruct((SUBLANES, n), F32),
        jax.ShapeDtypeStruct((1, LANES), F32),
    ]
    scratch = [
        pltpu.VMEM((RET_HEADS, HEAD_DIM, HEAD_DIM), F32),
        pltpu.VMEM((tm + SUBLANES, CONV_WIDTH), F32),
        pltpu.VMEM((tm, RET_WIDTH), BF16),
        pltpu.VMEM((1, LANES), F32),
        pltpu.VMEM((tm, IN_COLS), F32),
        pltpu.VMEM((tm, IN_COLS), F32),
        pltpu.VMEM((tm, D), BF16),
    ]
    return pl.pallas_call(
        functools.partial(_mixer_router_body, chunk_decay=chunk_decay, tiles_per_seq=tiles_per_seq),
        grid=(n_tiles,), in_specs=in_specs, out_specs=out_specs, out_shape=out_shape,
        scratch_shapes=scratch,
        compiler_params=pltpu.CompilerParams(
            dimension_semantics=("arbitrary",), vmem_limit_bytes=VMEM_LIMIT),
        name="mixer_router",
    )(x, x, n1, win_bf, cosf, sins, dmat, kdec, qdec, gnw, convw, wo_bf, n2, wr2, rb)


def _row_copy(src_ref, src_row, dst_ref, dst_row, sem):
    return pltpu.make_async_copy(_row_slab(src_ref, src_row), _row_slab(dst_ref, dst_row), sem)


def _zero_block_copy(zbuf, dst_ref, blk, sem):
    r0 = pl.multiple_of(blk * _quad(MOE_BLOCK), _quad(MOE_BLOCK))
    return pltpu.make_async_copy(zbuf, dst_ref.at[pl.ds(r0, _quad(MOE_BLOCK))], sem)


def _dispatch_body(dest0_ref, dest1_ref, fill_ref, h2p_ref, xs_ref, zbuf, sem, z_sem):
    tt = dest0_ref.shape[0]

    @pl.when(pl.program_id(0) == 0)
    def _():
        _zero_rows(zbuf)
        used = fill_ref[N_EXPERTS]
        total = xs_ref.shape[0] // _quad(MOE_BLOCK)

        def expert_pad(e, carry):
            blk = fill_ref[e]

            @pl.when(blk >= 0)
            def _():
                _zero_block_copy(zbuf, xs_ref, blk, z_sem).start()
            return carry

        def expert_pad_wait(e, carry):
            @pl.when(fill_ref[e] >= 0)
            def _():
                _zero_block_copy(zbuf, xs_ref, 0, z_sem).wait()
            return carry

        def tail(blk, carry):
            _zero_block_copy(zbuf, xs_ref, blk, z_sem).start()
            return carry

        def tail_wait(blk, carry):
            _zero_block_copy(zbuf, xs_ref, 0, z_sem).wait()
            return carry

        lax.fori_loop(0, N_EXPERTS, expert_pad, 0)
        lax.fori_loop(used, total, tail, 0)
        lax.fori_loop(0, N_EXPERTS, expert_pad_wait, 0)
        lax.fori_loop(used, total, tail_wait, 0)

    def issue(t, carry):
        _row_copy(h2p_ref, t, xs_ref, dest0_ref[t], sem).start(priority=0)
        _row_copy(h2p_ref, t, xs_ref, dest1_ref[t], sem).start(priority=1)
        return carry

    lax.fori_loop(0, tt, issue, 0, unroll=8)

    def drain(t, carry):
        _row_copy(h2p_ref, 0, xs_ref, 0, sem).wait()
        _row_copy(h2p_ref, 0, xs_ref, 0, sem).wait()
        return carry

    lax.fori_loop(0, tt, drain, 0, unroll=8)


def _dispatch(dest0, dest1, fill, h2p, xs_rows):
    n = dest0.shape[0]
    assert h2p.shape == (_quad(n), LANES)
    tt = ROW_TILE
    return pl.pallas_call(
        _dispatch_body,
        grid=(n // tt,),
        in_specs=[
            pl.BlockSpec((tt,), lambda i: (i,), memory_space=pltpu.SMEM),
            pl.BlockSpec((tt,), lambda i: (i,), memory_space=pltpu.SMEM),
            pl.BlockSpec(memory_space=pltpu.SMEM),
            pl.BlockSpec((_quad(tt), LANES), lambda i: (i, 0)),
        ],
        out_specs=pl.BlockSpec(memory_space=pl.ANY),
        out_shape=jax.ShapeDtypeStruct((_quad(xs_rows), LANES), U32),
        scratch_shapes=[
            pltpu.VMEM((_quad(MOE_BLOCK), LANES), U32),
            pltpu.SemaphoreType.DMA,
            pltpu.SemaphoreType.DMA,
        ],
        compiler_params=pltpu.CompilerParams(dimension_semantics=("arbitrary",)),
        name="dispatch",
    )(dest0, dest1, fill, h2p)


def _experts_body(ord_ref, row0_ref, nblk_ref, oe_ref, meta_ref, xs_ref, wg_ref, wu_ref, wd_ref, ys_ref,
                  xbuf, obuf, zbuf, wg_ring, wu_ring, wd_ring, wgb_ref, wub_ref, wdb_ref,
                  in_sem, out_sem, z_sem, w_sem):
    i = pl.program_id(0)
    n_chunks = meta_ref[0]
    n_ord = meta_ref[2]
    slot = lax.rem(i, 2)

    def weight_copies(k, ws):
        e = oe_ref[k]
        return (pltpu.make_async_copy(wg_ref.at[e], wg_ring.at[ws], w_sem.at[ws]),
                pltpu.make_async_copy(wu_ref.at[e], wu_ring.at[ws], w_sem.at[ws]),
                pltpu.make_async_copy(wd_ref.at[e], wd_ring.at[ws], w_sem.at[ws]))

    def fetch_weights(k):
        for c in weight_copies(k, lax.rem(k, WEIGHT_SLOTS)):
            c.start()

    def in_copy(chunk, s, piece):
        r0 = pl.multiple_of(_quad(row0_ref[chunk] + piece * MOE_BLOCK), _quad(MOE_BLOCK))
        return pltpu.make_async_copy(xs_ref.at[pl.ds(r0, _quad(MOE_BLOCK))],
                                     xbuf.at[s, pl.ds(piece * _quad(MOE_BLOCK), _quad(MOE_BLOCK))], in_sem.at[s])

    def out_copy(chunk, s, piece):
        r0 = pl.multiple_of(_quad(row0_ref[chunk] + piece * MOE_BLOCK), _quad(MOE_BLOCK))
        return pltpu.make_async_copy(obuf.at[s, pl.ds(piece * _quad(MOE_BLOCK), _quad(MOE_BLOCK))],
                                     ys_ref.at[pl.ds(r0, _quad(MOE_BLOCK))], out_sem.at[s])

    def for_pieces(chunk, fn):
        for piece in range(CHUNK_BLOCKS):
            @pl.when(piece < nblk_ref[chunk])
            def _():
                fn(piece)

    @pl.when(i == 0)
    def _():
        for_pieces(0, lambda piece: in_copy(0, 0, piece).start())
        for k in range(WEIGHT_SLOTS - 1):
            @pl.when(k < n_ord)
            def _():
                fetch_weights(k)

    @pl.when(i + 1 < n_chunks)
    def _():
        for_pieces(i + 1, lambda piece: in_copy(i + 1, 1 - slot, piece).start())

    @pl.when(i < n_chunks)
    def _():
        k = ord_ref[i]

        @pl.when((i == 0) | (k != ord_ref[jnp.maximum(i - 1, 0)]))
        def _():
            @pl.when(k + WEIGHT_SLOTS - 1 < n_ord)
            def _():
                fetch_weights(k + WEIGHT_SLOTS - 1)

            ws = lax.rem(k, WEIGHT_SLOTS)
            for c in weight_copies(k, ws):
                c.wait()
            wgb_ref[...] = wg_ring[ws].astype(BF16)
            wub_ref[...] = wu_ring[ws].astype(BF16)
            wdb_ref[...] = wd_ring[ws].astype(BF16)

        for_pieces(i, lambda piece: in_copy(i, slot, piece).wait())

        @pl.when(i >= 2)
        def _():
            for_pieces(i - 2, lambda piece: out_copy(i - 2, slot, piece).wait())

        for m in range(1, CHUNK_BLOCKS + 1):
            @pl.when(nblk_ref[i] == m)
            def _():
                rows = m * MOE_BLOCK
                los, his = _load_rows(xbuf, (slot,), rows)
                xb = jnp.concatenate([v.astype(BF16) for v in los + his], axis=-1)
                g = jnp.dot(xb, wgb_ref[...], preferred_element_type=F32)
                u = jnp.dot(xb, wub_ref[...], preferred_element_type=F32)
                hmid = (g * (1.0 / (1.0 + jnp.exp(-g)))) * u
                y = jnp.dot(hmid.astype(BF16), wdb_ref[...], preferred_element_type=F32)
                packed = _pack_bf16_pair(y[:, :PACKED], y[:, PACKED:])
                _store_rows(obuf, (slot,), rows, packed)

        for_pieces(i, lambda piece: out_copy(i, slot, piece).start())

    @pl.when(i == n_chunks - 1)
    def _():
        @pl.when(i >= 1)
        def _():
            for_pieces(i - 1, lambda piece: out_copy(i - 1, 1 - slot, piece).wait())
        for_pieces(i, lambda piece: out_copy(i, slot, piece).wait())

        _zero_rows(zbuf)
        used = meta_ref[1]
        total = ys_ref.shape[0] // _quad(MOE_BLOCK)

        def z_start(blk, carry):
            _zero_block_copy(zbuf, ys_ref, blk, z_sem).start()
            return carry

        def z_wait(blk, carry):
            _zero_block_copy(zbuf, ys_ref, blk, z_sem).wait()
            return carry

        lax.fori_loop(used, total, z_start, 0)
        lax.fori_loop(used, total, z_wait, 0)


def _experts(chunk_ord, chunk_row0, chunk_nblk, ord_expert, meta, xs, n_rows_out, w_gate, w_up, w_down):
    any_space = pl.BlockSpec(memory_space=pl.ANY)
    return pl.pallas_call(
        _experts_body,
        grid_spec=pltpu.PrefetchScalarGridSpec(
            num_scalar_prefetch=5,
            grid=(chunk_ord.shape[0],),
            in_specs=[any_space, any_space, any_space, any_space],
            out_specs=any_space,
            scratch_shapes=[
                pltpu.VMEM((2, _quad(CHUNK_ROWS), LANES), U32),
                pltpu.VMEM((2, _quad(CHUNK_ROWS), LANES), U32),
                pltpu.VMEM((_quad(MOE_BLOCK), LANES), U32),
                pltpu.VMEM((WEIGHT_SLOTS, D_MODEL, EXPERT_FF), F32),
                pltpu.VMEM((WEIGHT_SLOTS, D_MODEL, EXPERT_FF), F32),
                pltpu.VMEM((WEIGHT_SLOTS, EXPERT_FF, D_MODEL), F32),
                pltpu.VMEM((D_MODEL, EXPERT_FF), BF16),
                pltpu.VMEM((D_MODEL, EXPERT_FF), BF16),
                pltpu.VMEM((EXPERT_FF, D_MODEL), BF16),
                pltpu.SemaphoreType.DMA((2,)),
                pltpu.SemaphoreType.DMA((2,)),
                pltpu.SemaphoreType.DMA,
                pltpu.SemaphoreType.DMA((WEIGHT_SLOTS,)),
            ],
        ),
        out_shape=jax.ShapeDtypeStruct((_quad(n_rows_out), LANES), U32),
        compiler_params=pltpu.CompilerParams(
            dimension_semantics=("arbitrary",), vmem_limit_bytes=VMEM_LIMIT),
        name="experts",
    )(chunk_ord, chunk_row0, chunk_nblk, ord_expert, meta, xs, w_gate, w_up, w_down)


def _combine_body(dest0_ref, dest1_ref, next0_ref, next1_ref, x1_ref, route_ref, fw_ref, ys_ref, out_ref,
                  y0_ref, y1_ref, sem):
    i = pl.program_id(0)
    tt = x1_ref.shape[0]
    slot = lax.rem(i, 2)

    def gather(d0_ref, d1_ref, s):
        def issue(t, carry):
            _row_copy(ys_ref, d0_ref[t], y0_ref.at[s], t, sem.at[s]).start(priority=0)
            _row_copy(ys_ref, d1_ref[t], y1_ref.at[s], t, sem.at[s]).start(priority=1)
            return carry

        lax.fori_loop(0, tt, issue, 0, unroll=8)

    @pl.when(i == 0)
    def _():
        gather(dest0_ref, dest1_ref, 0)

    @pl.when(i + 1 < pl.num_programs(0))
    def _():
        gather(next0_ref, next1_ref, 1 - slot)

    def drain(t, carry):
        _row_copy(ys_ref, 0, y0_ref.at[slot], 0, sem.at[slot]).wait()
        _row_copy(ys_ref, 0, y1_ref.at[slot], 0, sem.at[slot]).wait()
        return carry

    lax.fori_loop(0, tt, drain, 0, unroll=8)

    route = route_ref[...]
    g0 = route[:, 2:3]
    g1 = route[:, 3:4]
    a_lo, a_hi = (jnp.concatenate(v, axis=-1) for v in _load_rows(y0_ref, (slot,), tt))
    b_lo, b_hi = (jnp.concatenate(v, axis=-1) for v in _load_rows(y1_ref, (slot,), tt))
    x1 = x1_ref[...]
    z_lo = x1[:, :PACKED] + (a_lo * g0 + b_lo * g1)
    z_hi = x1[:, PACKED:] + (a_hi * g0 + b_hi * g1)
    ms = (jnp.sum(z_lo * z_lo, axis=-1, keepdims=True)
          + jnp.sum(z_hi * z_hi, axis=-1, keepdims=True)) * (1.0 / D_MODEL)
    inv = lax.rsqrt(ms + EPS)
    out_ref[:, :PACKED] = (z_lo * inv) * fw_ref[:, :PACKED]
    out_ref[:, PACKED:] = (z_hi * inv) * fw_ref[:, PACKED:]


def _combine(dest0, dest1, x1, route, fw, ys):
    n, d = x1.shape
    tt = ROW_TILE
    steps = n // tt
    this_tile = lambda i: (i,)
    next_tile = lambda i: (jnp.minimum(i + 1, steps - 1),)
    return pl.pallas_call(
        _combine_body,
        grid=(steps,),
        in_specs=[
            pl.BlockSpec((tt,), this_tile, memory_space=pltpu.SMEM),
            pl.BlockSpec((tt,), this_tile, memory_space=pltpu.SMEM),
            pl.BlockSpec((tt,), next_tile, memory_space=pltpu.SMEM),
            pl.BlockSpec((tt,), next_tile, memory_space=pltpu.SMEM),
            pl.BlockSpec((tt, d), lambda i: (i, 0)),
            pl.BlockSpec((tt, SUBLANES), lambda i: (i, 0)),
            pl.BlockSpec((1, d), lambda i: (0, 0)),
            pl.BlockSpec(memory_space=pl.ANY),
        ],
        out_specs=pl.BlockSpec((tt, d), lambda i: (i, 0)),
        out_shape=jax.ShapeDtypeStruct((n, d), F32),
        scratch_shapes=[
            pltpu.VMEM((2, _quad(tt), LANES), U32),
            pltpu.VMEM((2, _quad(tt), LANES), U32),
            pltpu.SemaphoreType.DMA((2,)),
        ],
        compiler_params=pltpu.CompilerParams(
            dimension_semantics=("arbitrary",), vmem_limit_bytes=VMEM_LIMIT),
        name="combine",
    )(dest0, dest1, dest0, dest1, x1, route, fw, ys)


def _retention_tables(seq):
    half = HEAD_DIM // 2
    inv = ROPE_BASE ** (-jnp.arange(half, dtype=F32) / half)
    ang = jnp.arange(seq).astype(F32)[:, None] * inv[None, :]
    cos, sin = jnp.cos(ang), jnp.sin(ang)
    cosf = jnp.concatenate([cos, cos], axis=-1)
    sins = jnp.concatenate([-sin, sin], axis=-1)
    log_g = jnp.log1p(-(2.0 ** (-5.0 - jnp.arange(RET_HEADS, dtype=F32))))
    idx = jnp.arange(CHUNK, dtype=F32)
    rel = idx[:, None] - idx[None, :]
    causal = rel >= 0
    scale = HEAD_DIM ** -0.5
    dmat = jnp.where(causal[None], jnp.exp(log_g[:, None, None] * jnp.where(causal, rel, 0.0)[None]), 0.0)
    k_decay = jnp.exp(log_g[:, None] * (CHUNK - 1 - idx)[None, :])
    q_decay = jnp.exp(log_g[:, None] * (idx + 1)[None, :])
    kdec = jnp.repeat(k_decay.T, HEAD_DIM, axis=1)
    qdec = jnp.repeat(q_decay.T, HEAD_DIM, axis=1) * scale
    return cosf, sins, dmat * scale, kdec, qdec


def _chunk_decay():
    log_g = np.log1p(-(2.0 ** (-5.0 - np.arange(RET_HEADS, dtype=np.float64))))
    return tuple(float(v) for v in np.exp(log_g * CHUNK))


def _layer(x, norm1_w, w_in, ret_gn_w, conv_w, w_o, norm2_w, router_g_w, router_g_b,
           router_e_w, router_e_b, w_gate, w_up, w_down, final_w):
    B, S, D = x.shape
    n = B * S
    cosf, sins, dmat, kdec, qdec = _retention_tables(S)

    spare = LANES - N_GROUPS - N_EXPERTS
    wr = jnp.concatenate([router_g_w, jnp.transpose(router_e_w, (1, 0, 2)).reshape(D, N_EXPERTS),
                          jnp.zeros((D, spare), F32)], axis=1)
    wrh = wr.astype(BF16)
    wrl = (wr - wrh.astype(F32)).astype(BF16)
    wr2 = jnp.concatenate([wrh, wrl], axis=1)
    rb = jnp.concatenate([router_g_b, router_e_b.reshape(-1), jnp.zeros((spare,), F32)])[None]

    x1, h2p, route, route_t, counts = _mixer_router(
        x, norm1_w[None], w_in.astype(BF16), cosf, sins, dmat, kdec, qdec, ret_gn_w[None], conv_w,
        w_o.astype(BF16), norm2_w[None], wr2, rb, _chunk_decay())

    expert = route_t[0:2].astype(I32)
    rank = route_t[4:6].astype(I32)
    cnt = counts[0, EXPERT_LANE0:EXPERT_LANE0 + N_EXPERTS].astype(I32)
    padded = ((cnt + MOE_BLOCK - 1) // MOE_BLOCK) * MOE_BLOCK
    pad_ends = jnp.cumsum(padded)
    pad_starts = pad_ends - padded
    eids = jnp.arange(N_EXPERTS, dtype=I32)
    start_of = jnp.sum(jnp.where(expert[:, :, None] == eids, pad_starts, 0), axis=-1)
    dest = start_of + rank
    n_blocks = -(-(n * TOP_K) // MOE_BLOCK) + N_EXPERTS
    used_blocks = pad_ends[-1] // MOE_BLOCK

    n_chunk_slots = -(-(n * TOP_K) // CHUNK_ROWS) + N_EXPERTS
    chunks_of = (padded + CHUNK_ROWS - 1) // CHUNK_ROWS
    chunk_ends = jnp.cumsum(chunks_of)
    n_chunks = chunk_ends[-1]
    ci = jnp.minimum(jnp.arange(n_chunk_slots, dtype=I32), n_chunks - 1)
    chunk_e = jnp.minimum(jnp.sum((chunk_ends[None, :] <= ci[:, None]).astype(I32), axis=1), N_EXPERTS - 1)
    onehot_e = chunk_e[:, None] == eids
    pick = lambda v: jnp.sum(jnp.where(onehot_e, v, 0), axis=-1)
    k = ci - pick(chunk_ends - chunks_of)
    blocks_e = pick(padded) // MOE_BLOCK
    first_nblk = blocks_e - (pick(chunks_of) - 1) * CHUNK_BLOCKS
    chunk_nblk = jnp.where(k == 0, first_nblk, CHUNK_BLOCKS)
    chunk_row0 = pick(pad_starts) + jnp.where(k == 0, 0, first_nblk + (k - 1) * CHUNK_BLOCKS) * MOE_BLOCK
    has_rows = padded > 0
    ord_of = jnp.cumsum(has_rows.astype(I32)) - 1
    n_ord = jnp.sum(has_rows.astype(I32))
    chunk_ord = pick(ord_of)
    ord_expert = jnp.sum(jnp.where(has_rows[None, :] & (ord_of[None, :] == eids[:, None]), eids[None, :], 0), axis=1)
    meta = jnp.stack([n_chunks, used_blocks, n_ord]).astype(I32)

    xs_rows = n_blocks * MOE_BLOCK
    last_block = jnp.where(padded > 0, pad_ends // MOE_BLOCK - 1, -1)
    fill = jnp.concatenate([last_block, used_blocks[None]]).astype(I32)
    xs = _dispatch(dest[0], dest[1], fill, h2p, xs_rows)
    ys = _experts(chunk_ord.astype(I32), chunk_row0.astype(I32), chunk_nblk.astype(I32),
                  ord_expert.astype(I32), meta, xs, n_blocks * MOE_BLOCK, w_gate, w_up, w_down)
    out = _combine(dest[0], dest[1], x1, route, final_w[None], ys)
    return out.reshape(B, S, D)


def kernel(x, norm1_w, w_in, ret_gn_w, conv_w, w_o, norm2_w, router_g_w, router_g_b, router_e_w,
           router_e_b, w_gate, w_up, w_down, final_norm_w):
    depth = norm1_w.shape[0]
    assert depth == 1, "the final RMSNorm is fused into the last layer's combine step"
    return _layer(x, norm1_w[0], w_in[0], ret_gn_w[0], conv_w[0], w_o[0], norm2_w[0],
                  router_g_w[0], router_g_b[0], router_e_w[0], router_e_b[0],
                  w_gate[0], w_up[0], w_down[0], final_norm_w)
```
